```python
import jax
import jax.numpy as jnp
from jax import lax
import numpy as np

D_MODEL = 1024
BATCH = 32
SEQ = 2048
DEPTH = 2

N_EVEN = (DEPTH + 1) // 2
N_ODD = DEPTH // 2
EPS = 1e-6

GM_HEADS = 8
GM_HEAD_DIM = D_MODEL // (2 * GM_HEADS)
GM_WIDTH = GM_HEADS * GM_HEAD_DIM
GM_CHUNK = 128

MLA_HEADS = 8
MLA_Q_RANK = 3 * D_MODEL // 8
MLA_KV_RANK = D_MODEL // 4
MLA_NOPE_DIM = 64
MLA_ROPE_DIM = 32
MLA_QK_DIM = MLA_NOPE_DIM + MLA_ROPE_DIM
MLA_V_DIM = D_MODEL // (2 * MLA_HEADS)
MLA_WIDTH = MLA_HEADS * MLA_V_DIM
ROPE_THETA = 10000.0
ATTN_BLOCK = 128

IN_WIDTH = 2 * GM_WIDTH + MLA_Q_RANK + MLA_KV_RANK + MLA_ROPE_DIM
MIX_WIDTH = GM_WIDTH + MLA_WIDTH

CONV_CH = D_MODEL
CONV_WIDTH = 31

PEER_HEADS = 8
PEER_N_KEYS = 128
PEER_N_EXPERTS = PEER_N_KEYS * PEER_N_KEYS
PEER_KEY_DIM = 256
PEER_HALF = PEER_KEY_DIM // 2
PEER_TOPK = 16
PEER_BLOCK = 128

kernel_name = 'hybrid_gmlp_mla_conformer_peer'


def rms_norm(x, g):
    xf = x.astype(jnp.float32)
    y = xf * lax.rsqrt(jnp.mean(xf * xf, axis=-1, keepdims=True) + EPS)
    return (y * g.astype(jnp.float32)).astype(x.dtype)


def apply_rope(x, positions):
    half = MLA_ROPE_DIM // 2
    inv_freq = ROPE_THETA ** (-jnp.arange(half, dtype=jnp.float32) / half)
    ang = positions.astype(jnp.float32)[..., None] * inv_freq
    cos = jnp.cos(ang)[:, :, None, :]
    sin = jnp.sin(ang)[:, :, None, :]
    xf = x.astype(jnp.float32)
    x1, x2 = xf[..., :half], xf[..., half:]
    out = jnp.concatenate([x1 * cos - x2 * sin, x1 * sin + x2 * cos], axis=-1)
    return out.astype(x.dtype)


def chunked_spatial_gating(u, v, v_gain, w_s, b_s):
    B, S, H, Dh = u.shape
    n_chunks = S // GM_CHUNK
    u = jax.nn.gelu(u)
    v = rms_norm(jax.nn.gelu(v), v_gain)
    causal = jnp.tril(jnp.ones((GM_CHUNK, GM_CHUNK), dtype=bool))
    w = jnp.where(causal[None], w_s, 0).astype(v.dtype)
    vc = v.reshape(B, n_chunks, GM_CHUNK, H, Dh)
    mixed = jnp.einsum('hts,bcshd->bcthd', w, vc) + b_s.T.astype(v.dtype)[None, None, :, :, None]
    return (u * mixed.reshape(B, S, H, Dh)).reshape(B, S, H * Dh)


def causal_block_attention(q, k, v):
    S = q.shape[1]
    scale = MLA_QK_DIM ** -0.5
    outs = []
    for blk in range(S // ATTN_BLOCK):
        q0 = blk * ATTN_BLOCK
        kend = q0 + ATTN_BLOCK
        s = jnp.einsum('bqhd,bkhd->bhqk', q[:, q0:kend], k[:, :kend],
                       preferred_element_type=jnp.float32) * scale
        causal = (q0 + jnp.arange(ATTN_BLOCK))[:, None] >= jnp.arange(kend)[None, :]
        s = jnp.where(causal, s, -1e30)
        p = jax.nn.softmax(s, axis=-1).astype(v.dtype)
        outs.append(jnp.einsum('bhqk,bkhd->bqhd', p, v[:, :kend]))
    return jnp.concatenate(outs, axis=1)


def latent_attention(q_lat, kv_lat, k_pe, positions, q_lat_gain, w_uq, kv_lat_gain, w_ukv,
                     q_gain, k_gain):
    B, S, _ = q_lat.shape
    q = (rms_norm(q_lat, q_lat_gain) @ w_uq).reshape(B, S, MLA_HEADS, MLA_QK_DIM)
    kv = (rms_norm(kv_lat, kv_lat_gain) @ w_ukv).reshape(B, S, MLA_HEADS, MLA_NOPE_DIM + MLA_V_DIM)
    k_nope, v = kv[..., :MLA_NOPE_DIM], kv[..., MLA_NOPE_DIM:]
    k_rope = jnp.broadcast_to(k_pe[:, :, None, :], (B, S, MLA_HEADS, MLA_ROPE_DIM))
    k = jnp.concatenate([k_nope, k_rope], axis=-1)
    q = rms_norm(q, q_gain)
    k = rms_norm(k, k_gain)
    q = jnp.concatenate([q[..., :MLA_NOPE_DIM], apply_rope(q[..., MLA_NOPE_DIM:], positions)], axis=-1)
    k = jnp.concatenate([k[..., :MLA_NOPE_DIM], apply_rope(k[..., MLA_NOPE_DIM:], positions)], axis=-1)
    o = causal_block_attention(q, k, v)
    return o.reshape(B, S, MLA_WIDTH)


def parallel_gmlp_mla(h, positions, w_in, gm_v_gain, gm_w_s, gm_b_s, mla_q_lat_gain, mla_w_uq,
                      mla_kv_lat_gain, mla_w_ukv, mla_q_gain, mla_k_gain, w_out):
    B, S, _ = h.shape
    z = h @ w_in
    cuts = [GM_WIDTH, 2 * GM_WIDTH, 2 * GM_WIDTH + MLA_Q_RANK,
            2 * GM_WIDTH + MLA_Q_RANK + MLA_KV_RANK]
    u, v, q_lat, kv_lat, k_pe = jnp.split(z, cuts, axis=-1)
    a = chunked_spatial_gating(u.reshape(B, S, GM_HEADS, GM_HEAD_DIM),
                               v.reshape(B, S, GM_HEADS, GM_HEAD_DIM), gm_v_gain, gm_w_s, gm_b_s)
    b = latent_attention(q_lat, kv_lat, k_pe, positions, mla_q_lat_gain, mla_w_uq,
                         mla_kv_lat_gain, mla_w_ukv, mla_q_gain, mla_k_gain)
    return jnp.concatenate([a, b], axis=-1) @ w_out


def conformer_conv(h, w_pw1, b_pw1, w_dw, b_dw, norm_gain, w_pw2, b_pw2):
    z = h @ w_pw1 + b_pw1
    a, g = jnp.split(z, 2, axis=-1)
    z = a * jax.nn.sigmoid(g)
    z = lax.conv_general_dilated(z, w_dw[:, None, :].astype(z.dtype), window_strides=(1,),
                                 padding=[(CONV_WIDTH - 1, 0)],
                                 dimension_numbers=('NWC', 'WIO', 'NWC'),
                                 feature_group_count=CONV_CH) + b_dw
    z = jax.nn.silu(rms_norm(z, norm_gain))
    return z @ w_pw2 + b_pw2


def peer_layer(h, w_q, sub_keys, expert_u, expert_v):
    B, S, D = h.shape
    hb = h.reshape((B * S) // PEER_BLOCK, PEER_BLOCK, D)

    def retrieve(xb):
        T = xb.shape[0]
        q = (xb @ w_q).reshape(T, PEER_HEADS, 2, PEER_HALF)
        s = jnp.einsum('thcd,chnd->thcn', q, sub_keys, preferred_element_type=jnp.float32)
        top_s, top_i = lax.top_k(s, PEER_TOPK)
        cand_s = top_s[..., 0, :, None] + top_s[..., 1, None, :]
        cand_i = top_i[..., 0, :, None] * PEER_N_KEYS + top_i[..., 1, None, :]
        best_s, best_pos = lax.top_k(cand_s.reshape(T, PEER_HEADS, PEER_TOPK * PEER_TOPK), PEER_TOPK)
        idx = jnp.take_along_axis(cand_i.reshape(T, PEER_HEADS, PEER_TOPK * PEER_TOPK), best_pos, axis=-1)
        gate = jax.nn.softmax(best_s, axis=-1).astype(xb.dtype)
        act = jax.nn.gelu(jnp.einsum('thkd,td->thk', expert_u[idx], xb))
        return jnp.einsum('thk,thkd->td', gate * act, expert_v[idx])

    return lax.map(retrieve, hb).reshape(B, S, D)


def setup_inputs(seed: int = 0) -> dict:
    key = jax.random.key(seed)
    ks = iter(jax.random.split(key, 40))

    def nrm(shape, scale):
        return jax.random.normal(next(ks), shape, jnp.float32) * scale

    def gain(shape):
        return 1.0 + 0.05 * jax.random.normal(next(ks), shape, jnp.float32)

    x = jax.random.normal(next(ks), (BATCH, SEQ, D_MODEL), jnp.float32)
    offsets = jax.random.randint(next(ks), (BATCH, 1), 0, 4096, dtype=jnp.int32)
    positions = offsets + jnp.arange(SEQ, dtype=jnp.int32)[None, :]
    return {
        'x': x,
        'positions': positions,
        'mix_norm_even': gain((N_EVEN, D_MODEL)),
        'w_in': nrm((N_EVEN, D_MODEL, IN_WIDTH), D_MODEL ** -0.5),
        'gm_v_gain': gain((N_EVEN, GM_HEADS, GM_HEAD_DIM)),
        'gm_w_s': nrm((N_EVEN, GM_HEADS, GM_CHUNK, GM_CHUNK), GM_CHUNK ** -0.5),
        'gm_b_s': 1.0 + 0.1 * nrm((N_EVEN, GM_HEADS, GM_CHUNK), 1.0),
        'mla_q_lat_gain': gain((N_EVEN, MLA_Q_RANK)),
        'mla_w_uq': nrm((N_EVEN, MLA_Q_RANK, MLA_HEADS * MLA_QK_DIM), MLA_Q_RANK ** -0.5),
        'mla_kv_lat_gain': gain((N_EVEN, MLA_KV_RANK)),
        'mla_w_ukv': nrm((N_EVEN, MLA_KV_RANK, MLA_HEADS * (MLA_NOPE_DIM + MLA_V_DIM)), MLA_KV_RANK ** -0.5),
        'mla_q_gain': gain((N_EVEN, MLA_QK_DIM)),
        'mla_k_gain': gain((N_EVEN, MLA_QK_DIM)),
        'w_out': nrm((N_EVEN, MIX_WIDTH, D_MODEL), MIX_WIDTH ** -0.5),
        'mix_norm_odd': gain((N_ODD, D_MODEL)),
        'conv_w_pw1': nrm((N_ODD, D_MODEL, 2 * CONV_CH), D_MODEL ** -0.5),
        'conv_b_pw1': nrm((N_ODD, 2 * CONV_CH), 0.02),
        'conv_w_dw': nrm((N_ODD, CONV_WIDTH, CONV_CH), CONV_WIDTH ** -0.5),
        'conv_b_dw': nrm((N_ODD, CONV_CH), 0.02),
        'conv_norm_gain': gain((N_ODD, CONV_CH)),
        'conv_w_pw2': nrm((N_ODD, CONV_CH, D_MODEL), CONV_CH ** -0.5),
        'conv_b_pw2': nrm((N_ODD, D_MODEL), 0.02),
        'ffn_norm': gain((DEPTH, D_MODEL)),
        'peer_w_q': nrm((DEPTH, D_MODEL, PEER_HEADS * PEER_KEY_DIM), D_MODEL ** -0.5),
        'peer_sub_keys': nrm((DEPTH, 2, PEER_HEADS, PEER_N_KEYS, PEER_HALF), PEER_HALF ** -0.5),
        'peer_u': nrm((DEPTH, PEER_N_EXPERTS, D_MODEL), D_MODEL ** -0.5),
        'peer_v': nrm((DEPTH, PEER_N_EXPERTS, D_MODEL), PEER_HEADS ** -0.5),
    }


def reference(x, positions, mix_norm_even, w_in, gm_v_gain, gm_w_s, gm_b_s, mla_q_lat_gain,
              mla_w_uq, mla_kv_lat_gain, mla_w_ukv, mla_q_gain, mla_k_gain, w_out, mix_norm_odd,
              conv_w_pw1, conv_b_pw1, conv_w_dw, conv_b_dw, conv_norm_gain, conv_w_pw2, conv_b_pw2,
              ffn_norm, peer_w_q, peer_sub_keys, peer_u, peer_v):
    for layer in range(DEPTH):
        i = layer // 2
        if layer % 2 == 0:
            h = rms_norm(x, mix_norm_even[i])
            x = x + parallel_gmlp_mla(h, positions, w_in[i], gm_v_gain[i], gm_w_s[i], gm_b_s[i],
                                      mla_q_lat_gain[i], mla_w_uq[i], mla_kv_lat_gain[i],
                                      mla_w_ukv[i], mla_q_gain[i], mla_k_gain[i], w_out[i])
        else:
            h = rms_norm(x, mix_norm_odd[i])
            x = x + conformer_conv(h, conv_w_pw1[i], conv_b_pw1[i], conv_w_dw[i], conv_b_dw[i],
                                   conv_norm_gain[i], conv_w_pw2[i], conv_b_pw2[i])
        h = rms_norm(x, ffn_norm[layer])
        x = x + peer_layer(h, peer_w_q[layer], peer_sub_keys[layer], peer_u[layer], peer_v[layer])
    return x
```

```python
import functools
import math

import jax
import jax.numpy as jnp
from jax import lax
from jax.experimental import pallas as pl
from jax.experimental.pallas import tpu as pltpu

F32 = jnp.float32
MXU_DTYPE = jnp.bfloat16

D_MODEL = 1024
EPS = 1e-6
NEG = -1e30

GM_HEADS = 8
GM_HEAD_DIM = 64
GM_WIDTH = 512
GM_CHUNK = 128

MLA_HEADS = 8
MLA_Q_RANK = 384
MLA_KV_RANK = 256
MLA_NOPE_DIM = 64
MLA_ROPE_DIM = 32
MLA_QK_DIM = 96
MLA_V_DIM = 64
MLA_WIDTH = 512
ROPE_THETA = 10000.0
HEAD_PAD = 128

IN_WIDTH = 1696
IN_WIDTH_PAD = 1792

CONV_CH = 1024
CONV_WIDTH = 31
CONV_HALO = 32

PEER_HEADS = 8
PEER_N_KEYS = 128
PEER_HALF = 128
PEER_TOPK = 16

LANES = 128
VMEM_LIMIT = 56 * 1024 * 1024


def _dot(a, b):
    return jnp.dot(a, b, preferred_element_type=F32)


def _dot_nt(a, b):
    return lax.dot_general(a, b, (((1,), (1,)), ((), ())), preferred_element_type=F32)


def _mx(a):
    return a.astype(MXU_DTYPE)


def _gelu(x):
    c = math.sqrt(2.0 / math.pi)
    return 0.5 * x * (1.0 + jnp.tanh(c * (x + 0.044715 * (x * x * x))))


def _rms(x, g):
    ms = jnp.mean(x * x, axis=-1, keepdims=True)
    return x * lax.rsqrt(ms + EPS) * g


def _seg_sum(y, seg):
    hi = _mx(y)
    lo = _mx(y - hi.astype(F32))
    return _dot(hi, seg) + _dot(lo, seg)


def _params(n_parallel_axes, n_axes):
    sem = ("parallel",) * n_parallel_axes + ("arbitrary",) * (n_axes - n_parallel_axes)
    return pltpu.CompilerParams(dimension_semantics=sem, vmem_limit_bytes=VMEM_LIMIT)


def _full(shape):
    nd = len(shape)
    return pl.BlockSpec(shape, lambda *_: (0,) * nd)


def _mixer_in_kernel(x_ref, pos_ref, g_ref, win_ref, vseg_ref, vgain_ref, qlg_ref, wuq_ref, kvlg_ref,
                     wuk_ref, wuv_ref, eplace_ref, rot_ref, ones_ref, qg_ref, kg_ref, invf_ref,
                     ug_ref, vn_ref, q_ref, k_ref, v_ref):
    h = _mx(_rms(x_ref[...], g_ref[...]))
    z = _dot(h, win_ref[...])
    u = z[:, :GM_WIDTH]
    v = z[:, GM_WIDTH:2 * GM_WIDTH]
    q_lat = z[:, 2 * GM_WIDTH:2 * GM_WIDTH + MLA_Q_RANK]
    kv_lat = z[:, 2 * GM_WIDTH + MLA_Q_RANK:2 * GM_WIDTH + MLA_Q_RANK + MLA_KV_RANK]
    k_pe = z[:, IN_WIDTH_PAD - LANES:]

    ug_ref[...] = _gelu(u).astype(ug_ref.dtype)
    gv = _gelu(v)
    vseg = vseg_ref[...]
    for s in range(GM_WIDTH // LANES):
        sl = slice(s * LANES, (s + 1) * LANES)
        y = gv[:, sl]
        ms = _seg_sum(y * y, vseg) * (1.0 / GM_HEAD_DIM)
        vn_ref[:, sl] = (y * lax.rsqrt(ms + EPS) * vgain_ref[:, sl]).astype(vn_ref.dtype)

    ang = pos_ref[...].astype(F32) * invf_ref[...]
    cos = jnp.cos(ang)
    sin = jnp.sin(ang)
    rot = rot_ref[...]
    ones = ones_ref[...]

    def qk_norm_rope(raw, gain, scale):
        ms = _seg_sum(raw * raw, ones) * (1.0 / MLA_QK_DIM)
        y = raw * lax.rsqrt(ms + EPS) * gain
        return (y * cos + _dot(_mx(y), rot) * sin) * scale

    qn = _mx(_rms(q_lat, qlg_ref[...]))
    q_raw = _dot(qn, wuq_ref[...])
    kvn = _mx(_rms(kv_lat, kvlg_ref[...]))
    k_raw = _dot(kvn, wuk_ref[...]) + _dot(_mx(k_pe), eplace_ref[...])
    v_ref[...] = _dot(kvn, wuv_ref[...]).astype(v_ref.dtype)
    q_scale = MLA_QK_DIM ** -0.5
    for hd in range(MLA_HEADS):
        sl = slice(hd * HEAD_PAD, (hd + 1) * HEAD_PAD)
        q_ref[:, sl] = qk_norm_rope(q_raw[:, sl], qg_ref[...], q_scale).astype(q_ref.dtype)
        k_ref[:, sl] = qk_norm_rope(k_raw[:, sl], kg_ref[...], 1.0).astype(k_ref.dtype)


def _mixer_in(x2, pos2, g, w_in, gm_v_gain, q_lat_gain, w_uq, kv_lat_gain, w_ukv, q_gain, k_gain, tile):
    n = x2.shape[0]
    win = _mx(jnp.pad(w_in, ((0, 0), (0, IN_WIDTH_PAD - IN_WIDTH))))
    lane = jnp.arange(LANES)
    vseg = _mx((lane[:, None] // GM_HEAD_DIM == lane[None, :] // GM_HEAD_DIM).astype(F32))
    ones = _mx(jnp.ones((LANES, LANES), F32))
    vgain = gm_v_gain.reshape(1, GM_WIDTH)
    wuq = w_uq.reshape(MLA_Q_RANK, MLA_HEADS, MLA_QK_DIM)
    wuq = _mx(jnp.pad(wuq, ((0, 0), (0, 0), (0, HEAD_PAD - MLA_QK_DIM))).reshape(MLA_Q_RANK, MLA_HEADS * HEAD_PAD))
    wukv = w_ukv.reshape(MLA_KV_RANK, MLA_HEADS, MLA_NOPE_DIM + MLA_V_DIM)
    wuk = jnp.pad(wukv[:, :, :MLA_NOPE_DIM], ((0, 0), (0, 0), (0, HEAD_PAD - MLA_NOPE_DIM)))
    wuk = _mx(wuk.reshape(MLA_KV_RANK, MLA_HEADS * HEAD_PAD))
    wuv = _mx(wukv[:, :, MLA_NOPE_DIM:].reshape(MLA_KV_RANK, MLA_WIDTH))
    r = jnp.arange(MLA_ROPE_DIM)
    eplace = jnp.zeros((LANES, MLA_HEADS, HEAD_PAD), F32).at[r, :, MLA_NOPE_DIM + r].set(1.0)
    eplace = _mx(eplace.reshape(LANES, MLA_HEADS * HEAD_PAD))
    half = MLA_ROPE_DIM // 2
    j = jnp.arange(half)
    rot = jnp.zeros((HEAD_PAD, HEAD_PAD), F32)
    rot = rot.at[MLA_NOPE_DIM + half + j, MLA_NOPE_DIM + j].set(-1.0).at[MLA_NOPE_DIM + j, MLA_NOPE_DIM + half + j].set(1.0)
    rot = _mx(rot)
    inv_freq = ROPE_THETA ** (-jnp.arange(half, dtype=F32) / half)
    invf = jnp.zeros((1, HEAD_PAD), F32).at[0, MLA_NOPE_DIM + j].set(inv_freq).at[0, MLA_NOPE_DIM + half + j].set(inv_freq)
    qg = jnp.pad(q_gain, (0, HEAD_PAD - MLA_QK_DIM)).reshape(1, HEAD_PAD)
    kg = jnp.pad(k_gain, (0, HEAD_PAD - MLA_QK_DIM)).reshape(1, HEAD_PAD)

    row = lambda w: pl.BlockSpec((tile, w), lambda i: (i, 0))
    consts = (g.reshape(1, D_MODEL), win, vseg, vgain, q_lat_gain.reshape(1, -1), wuq, kv_lat_gain.reshape(1, -1),
              wuk, wuv, eplace, rot, ones, qg, kg, invf)
    out_dt = MXU_DTYPE
    return pl.pallas_call(
        _mixer_in_kernel,
        grid=(n // tile,),
        in_specs=[row(D_MODEL), row(1)] + [_full(c.shape) for c in consts],
        out_specs=[row(GM_WIDTH), row(GM_WIDTH), row(MLA_HEADS * HEAD_PAD), row(MLA_HEADS * HEAD_PAD), row(MLA_WIDTH)],
        out_shape=[jax.ShapeDtypeStruct((n, GM_WIDTH), out_dt), jax.ShapeDtypeStruct((n, GM_WIDTH), out_dt),
                   jax.ShapeDtypeStruct((n, MLA_HEADS * HEAD_PAD), out_dt),
                   jax.ShapeDtypeStruct((n, MLA_HEADS * HEAD_PAD), out_dt),
                   jax.ShapeDtypeStruct((n, MLA_WIDTH), out_dt)],
        compiler_params=_params(1, 1),
        name="mixer_in",
    )(x2, pos2, *consts)


def _attention_kernel(q_ref, k_ref, v_ref, o_ref, *, tq):
    qi = pl.program_id(2)
    lane = lax.broadcasted_iota(jnp.int32, (tq, LANES), 1)
    outs = []
    for hh in range(2):
        hs = slice(hh * HEAD_PAD, (hh + 1) * HEAD_PAD)
        q = q_ref[:, hs]

        def scores(j):
            kj = k_ref[pl.ds(pl.multiple_of(j * tq, tq), tq), hs]
            return _dot_nt(q, kj)

        def update(carry, s, j):
            m, l, acc = carry
            m_new = jnp.maximum(m, jnp.max(s, axis=-1, keepdims=True))
            alpha = jnp.exp(m - m_new)
            p = jnp.exp(s - m_new)
            vj = v_ref[pl.ds(pl.multiple_of(j * tq, tq), tq), :]
            acc = acc * alpha + _dot(_mx(p), vj)
            return m_new, l * alpha + jnp.sum(p, axis=-1, keepdims=True), acc

        def body(j, carry):
            return update(carry, scores(j), j)

        init = (jnp.full((tq, 1), NEG, F32), jnp.zeros((tq, 1), F32), jnp.zeros((tq, LANES), F32))
        carry = lax.fori_loop(0, qi, body, init)
        r = lax.broadcasted_iota(jnp.int32, (tq, tq), 0)
        c = lax.broadcasted_iota(jnp.int32, (tq, tq), 1)
        m, l, acc = update(carry, jnp.where(r >= c, scores(qi), NEG), qi)
        outs.append(acc / l)
    o_ref[...] = jnp.where(lane < MLA_V_DIM, outs[0], outs[1]).astype(o_ref.dtype)


def _attention(q, k, v, batch, seq, tq):
    n = batch * seq
    nq = seq // tq
    return pl.pallas_call(
        functools.partial(_attention_kernel, tq=tq),
        grid=(batch, MLA_HEADS // 2, nq),
        in_specs=[pl.BlockSpec((tq, 2 * HEAD_PAD), lambda b, hp, i: (b * nq + i, hp)),
                  pl.BlockSpec((seq, 2 * HEAD_PAD), lambda b, hp, i: (b, hp)),
                  pl.BlockSpec((seq, 2 * MLA_V_DIM), lambda b, hp, i: (b, hp))],
        out_specs=pl.BlockSpec((tq, 2 * MLA_V_DIM), lambda b, hp, i: (b * nq + i, hp)),
        out_shape=jax.ShapeDtypeStruct((n, MLA_WIDTH), MXU_DTYPE),
        compiler_params=_params(3, 3),
        name="attention",
    )(q, k, v)


def _mixer_out_kernel(ug_ref, vn_ref, o_ref, x_ref, ws_ref, bias_ref, woa_ref, wob_ref, out_ref, a_ref, *, tile):
    lane = lax.broadcasted_iota(jnp.int32, (GM_CHUNK, LANES), 1)
    left = lane < GM_HEAD_DIM
    for c in range(tile // GM_CHUNK):
        rows = slice(c * GM_CHUNK, (c + 1) * GM_CHUNK)
        for p in range(GM_WIDTH // LANES):
            sl = slice(p * LANES, (p + 1) * LANES)
            vs = vn_ref[rows, sl]
            zero = jnp.zeros_like(vs)
            rhs = jnp.concatenate([jnp.where(left, vs, zero), jnp.where(left, zero, vs)], axis=0)
            mixed = _dot(ws_ref[p], rhs) + bias_ref[:, sl]
            a_ref[rows, sl] = (ug_ref[rows, sl].astype(F32) * mixed).astype(a_ref.dtype)
    out_ref[...] = x_ref[...] + _dot(a_ref[...], woa_ref[...]) + _dot(o_ref[...], wob_ref[...])


def _mixer_out(ug, vn, o, x2, gm_w_s, gm_b_s, w_out, tile):
    n = x2.shape[0]
    causal = jnp.tril(jnp.ones((GM_CHUNK, GM_CHUNK), bool))
    w = jnp.where(causal[None], gm_w_s, 0.0)
    ws = _mx(w.reshape(GM_HEADS // 2, 2, GM_CHUNK, GM_CHUNK).transpose(0, 2, 1, 3).reshape(GM_HEADS // 2, GM_CHUNK, 2 * GM_CHUNK))
    bias = jnp.repeat(gm_b_s.T, GM_HEAD_DIM, axis=1)
    woa = _mx(w_out[:GM_WIDTH])
    wob = _mx(w_out[GM_WIDTH:])
    row = lambda w_: pl.BlockSpec((tile, w_), lambda i: (i, 0))
    return pl.pallas_call(
        functools.partial(_mixer_out_kernel, tile=tile),
        grid=(n // tile,),
        in_specs=[row(GM_WIDTH), row(GM_WIDTH), row(MLA_WIDTH), row(D_MODEL),
                  _full(ws.shape), _full(bias.shape), _full(woa.shape), _full(wob.shape)],
        out_specs=row(D_MODEL),
        out_shape=jax.ShapeDtypeStruct((n, D_MODEL), F32),
        scratch_shapes=[pltpu.VMEM((tile, GM_WIDTH), MXU_DTYPE)],
        compiler_params=_params(1, 1),
        name="mixer_out",
    )(ug, vn, o, x2, ws, bias, woa, wob)


def _conformer_kernel(x_ref, g_ref, w1_ref, b1_ref, wdw_ref, bdw_ref, ng_ref, w2_ref, b2_ref, out_ref, zext_ref, *, tile):
    @pl.when(pl.program_id(1) == 0)
    def _():
        zext_ref[0:CONV_HALO, :] = jnp.zeros((CONV_HALO, CONV_CH), F32)

    x = x_ref[...]
    h = _mx(_rms(x, g_ref[...]))
    z = _dot(h, w1_ref[...]) + b1_ref[...]
    glu = z[:, :CONV_CH] * jax.nn.sigmoid(z[:, CONV_CH:])
    zext_ref[CONV_HALO:CONV_HALO + tile, :] = glu
    first = CONV_HALO - (CONV_WIDTH - 1)
    cols = []
    for cb in range(CONV_CH // LANES):
        sl = slice(cb * LANES, (cb + 1) * LANES)
        acc = jnp.zeros((tile, LANES), F32)
        for k in range(CONV_WIDTH):
            acc = acc + zext_ref[first + k:first + k + tile, sl] * wdw_ref[k:k + 1, sl]
        cols.append(acc)
    y = jnp.concatenate(cols, axis=1) + bdw_ref[...]
    zext_ref[0:CONV_HALO, :] = zext_ref[tile:tile + CONV_HALO, :]
    yn = _rms(y, ng_ref[...])
    s = yn * jax.nn.sigmoid(yn)
    out_ref[...] = x + _dot(_mx(s), w2_ref[...]) + b2_ref[...]


def _conformer(x2, g, w_pw1, b_pw1, w_dw, b_dw, norm_gain, w_pw2, b_pw2, batch, seq, tile):
    n = x2.shape[0]
    nt = seq // tile
    consts = (g.reshape(1, -1), _mx(w_pw1), b_pw1.reshape(1, -1), w_dw, b_dw.reshape(1, -1),
              norm_gain.reshape(1, -1), _mx(w_pw2), b_pw2.reshape(1, -1))
    row = pl.BlockSpec((tile, D_MODEL), lambda b, i: (b * nt + i, 0))
    return pl.pallas_call(
        functools.partial(_conformer_kernel, tile=tile),
        grid=(batch, nt),
        in_specs=[row] + [_full(c.shape) for c in consts],
        out_specs=row,
        out_shape=jax.ShapeDtypeStruct((n, D_MODEL), F32),
        scratch_shapes=[pltpu.VMEM((CONV_HALO + tile, CONV_CH), F32)],
        compiler_params=_params(0, 2),
        name="conformer",
    )(x2, *consts)


def _top_values(x, k):
    vals = []
    for _ in range(k):
        m = jnp.max(x, axis=0, keepdims=True)
        vals.append(m)
        x = jnp.where(x == m, NEG, x)
    return jnp.concatenate(vals, axis=0)


def _peer_route_kernel(x_ref, g_ref, wqt_ref, keys_ref, ht_ref, s1_ref, s2_ref, e2_ref, coef_ref, tau_ref, *, tile):
    h = _rms(x_ref[...], g_ref[...])
    ht = _mx(h.T)
    ht_ref[...] = ht
    qt = _dot(wqt_ref[...], ht)
    row = lax.broadcasted_iota(jnp.int32, (PEER_TOPK, LANES), 0)
    for hd in range(PEER_HEADS):
        base = hd * 2 * PEER_HALF
        s1 = _dot(keys_ref[0, hd], _mx(qt[base:base + PEER_HALF]))
        s2 = _dot(keys_ref[1, hd], _mx(qt[base + PEER_HALF:base + 2 * PEER_HALF]))
        s1_ref[hd] = s1
        s2_ref[hd] = s2
        for lc in range(tile // LANES):
            ls = slice(lc * LANES, (lc + 1) * LANES)
            a1 = s1[:, ls]
            a2 = s2[:, ls]
            c1 = _top_values(a1, PEER_TOPK)
            c2 = _top_values(a2, PEER_TOPK)
            cand = [c1[a:a + 1] + c2 for a in range(4)]
            c1_rest = jnp.where(row < 4, NEG, c1)
            cand += [c2[b:b + 1] + c1_rest for b in range(4)]
            top = _top_values(jnp.concatenate(cand, axis=0), PEER_TOPK)
            zsum = jnp.sum(jnp.exp(top - top[0:1]), axis=0, keepdims=True)
            tau_ref[hd, :, ls] = top[PEER_TOPK - 1:PEER_TOPK]
            e2_ref[hd, :, ls] = jnp.exp(a2 - c2[0:1])
            coef_ref[hd, :, ls] = jnp.exp(a1 - c1[0:1]) / zsum


def _peer_route(x2, g, w_q, sub_keys, tile):
    n = x2.shape[0]
    wqt = _mx(w_q.T)
    keys = _mx(sub_keys)
    tok = pl.BlockSpec((PEER_HEADS, PEER_N_KEYS, tile), lambda i: (0, 0, i))
    stat = jax.ShapeDtypeStruct((PEER_HEADS, PEER_N_KEYS, n), F32)
    return pl.pallas_call(
        functools.partial(_peer_route_kernel, tile=tile),
        grid=(n // tile,),
        in_specs=[pl.BlockSpec((tile, D_MODEL), lambda i: (i, 0)), _full((1, D_MODEL)), _full(wqt.shape), _full(keys.shape)],
        out_specs=[pl.BlockSpec((D_MODEL, tile), lambda i: (0, i)), tok, tok, tok, tok,
                   pl.BlockSpec((PEER_HEADS, 1, tile), lambda i: (0, 0, i))],
        out_shape=[jax.ShapeDtypeStruct((D_MODEL, n), MXU_DTYPE), stat, stat, stat, stat,
                   jax.ShapeDtypeStruct((PEER_HEADS, 1, n), F32)],
        compiler_params=_params(1, 1),
        name="peer_route",
    )(x2, g.reshape(1, D_MODEL), wqt, keys)


def _peer_expert_kernel(ht_ref, u_ref, vt_ref, s1_ref, coef_ref, s2_ref, e2_ref, tau_ref, x_ref, out_ref,
                        acc_ref, g_ref, *, tile, ib):
    step = pl.program_id(1)

    @pl.when(step == 0)
    def _():
        acc_ref[...] = jnp.zeros_like(acc_ref)

    ht = ht_ref[...]

    for k in range(ib):
        rows = slice(k * PEER_N_KEYS, (k + 1) * PEER_N_KEYS)
        act = _gelu(_dot(u_ref[rows, :], ht))
        for lc in range(tile // LANES):
            ls = slice(lc * LANES, (lc + 1) * LANES)
            w = jnp.zeros((PEER_N_KEYS, LANES), F32)
            for hd in range(PEER_HEADS):
                zsum = s2_ref[hd, :, ls] + s1_ref[hd, k:k + 1, ls]
                w = w + jnp.where(zsum >= tau_ref[hd, :, ls], e2_ref[hd, :, ls], 0.0) * coef_ref[hd, k:k + 1, ls]
            g_ref[rows, ls] = (w * act[:, ls]).astype(g_ref.dtype)
    acc_ref[...] += _dot(vt_ref[...], g_ref[...])

    @pl.when(step == pl.num_programs(1) - 1)
    def _():
        out_ref[...] = x_ref[...] + acc_ref[...].T


def _peer_experts(x2, ht, s1, s2, e2, coef, tau, u, vt, tile, ib):
    n = x2.shape[0]
    blk = ib * PEER_N_KEYS
    small = pl.BlockSpec((PEER_HEADS, ib, tile), lambda i, j: (0, j, i))
    big = pl.BlockSpec((PEER_HEADS, PEER_N_KEYS, tile), lambda i, j: (0, 0, i))
    return pl.pallas_call(
        functools.partial(_peer_expert_kernel, tile=tile, ib=ib),
        grid=(n // tile, PEER_N_KEYS // ib),
        in_specs=[pl.BlockSpec((D_MODEL, tile), lambda i, j: (0, i)),
                  pl.BlockSpec((blk, D_MODEL), lambda i, j: (j, 0)),
                  pl.BlockSpec((D_MODEL, blk), lambda i, j: (0, j)),
                  small, small, big, big,
                  pl.BlockSpec((PEER_HEADS, 1, tile), lambda i, j: (0, 0, i)),
                  pl.BlockSpec((tile, D_MODEL), lambda i, j: (i, 0))],
        out_specs=pl.BlockSpec((tile, D_MODEL), lambda i, j: (i, 0)),
        out_shape=jax.ShapeDtypeStruct((n, D_MODEL), F32),
        scratch_shapes=[pltpu.VMEM((D_MODEL, tile), F32), pltpu.VMEM((blk, tile), MXU_DTYPE)],
        compiler_params=_params(1, 2),
        name="peer_experts",
    )(ht, u, vt, s1, coef, s2, e2, tau, x2)


def _peer(x2, g, w_q, sub_keys, expert_u, expert_v, route_tile, tile, ib):
    ht, s1, s2, e2, coef, tau = _peer_route(x2, g, w_q, sub_keys, route_tile)
    return _peer_experts(x2, ht, s1, s2, e2, coef, tau, _mx(expert_u), _mx(expert_v.T), tile, ib)


def _tile(n, want):
    t = min(n, want)
    assert n % t == 0, (n, t)
    return t


def kernel(x, positions, mix_norm_even, w_in, gm_v_gain, gm_w_s, gm_b_s, mla_q_lat_gain, mla_w_uq, mla_kv_lat_gain, mla_w_ukv, mla_q_gain, mla_k_gain, w_out, mix_norm_odd, conv_w_pw1, conv_b_pw1, conv_w_dw, conv_b_dw, conv_norm_gain, conv_w_pw2, conv_b_pw2, ffn_norm, peer_w_q, peer_sub_keys, peer_u, peer_v):
    batch, seq, d = x.shape
    n = batch * seq
    depth = ffn_norm.shape[0]
    x2 = x.reshape(n, d)
    pos2 = positions.reshape(n, 1)
    row_tile = _tile(n, 512)
    for layer in range(depth):
        i = layer // 2
        if layer % 2 == 0:
            ug, vn, q, k, v = _mixer_in(x2, pos2, mix_norm_even[i], w_in[i], gm_v_gain[i], mla_q_lat_gain[i],
                                        mla_w_uq[i], mla_kv_lat_gain[i], mla_w_ukv[i], mla_q_gain[i],
                                        mla_k_gain[i], row_tile)
            o = _attention(q, k, v, batch, seq, _tile(seq, 256))
            x2 = _mixer_out(ug, vn, o, x2, gm_w_s[i], gm_b_s[i], w_out[i], row_tile)
        else:
            x2 = _conformer(x2, mix_norm_odd[i], conv_w_pw1[i], conv_b_pw1[i], conv_w_dw[i], conv_b_dw[i],
                            conv_norm_gain[i], conv_w_pw2[i], conv_b_pw2[i], batch, seq, _tile(seq, 256))
        x2 = _peer(x2, ffn_norm[layer], peer_w_q[layer], peer_sub_keys[layer], peer_u[layer], peer_v[layer],
                   _tile(n, 256), row_tile, 8)
    return x2.reshape(batch, seq, d)
```

```python
import functools
import math

import jax
import jax.numpy as jnp
from jax import lax
from jax.experimental import pallas as pl
from jax.experimental.pallas import tpu as pltpu

F32 = jnp.float32
MXU_DTYPE = jnp.bfloat16

D_MODEL = 1024
EPS = 1e-6
NEG = -1e30

GM_HEADS = 8
GM_HEAD_DIM = 64
GM_WIDTH = 512
GM_CHUNK = 128

MLA_HEADS = 8
MLA_Q_RANK = 384
MLA_KV_RANK = 256
MLA_NOPE_DIM = 64
MLA_ROPE_DIM = 32
MLA_QK_DIM = 96
MLA_V_DIM = 64
MLA_WIDTH = 512
ROPE_THETA = 10000.0
HEAD_PAD = 128

IN_WIDTH = 1696
IN_WIDTH_PAD = 1792

CONV_CH = 1024
CONV_WIDTH = 31
CONV_HALO = 32

PEER_HEADS = 8
PEER_N_KEYS = 128
PEER_HALF = 128
PEER_TOPK = 16

LANES = 128
VMEM_LIMIT = 56 * 1024 * 1024


def _dot(a, b):
    return jnp.dot(a, b, preferred_element_type=F32)


def _dot_nt(a, b):
    return lax.dot_general(a, b, (((1,), (1,)), ((), ())), preferred_element_type=F32)


def _mx(a):
    return a.astype(MXU_DTYPE)


def _gelu(x):
    c = math.sqrt(2.0 / math.pi)
    return x * (0.5 + 0.5 * jnp.tanh(x * (c + (c * 0.044715) * (x * x))))


def _rms(x, g):
    ms = jnp.mean(x * x, axis=-1, keepdims=True)
    return x * lax.rsqrt(ms + EPS) * g


def _seg_sum(y, seg):
    hi = _mx(y)
    lo = _mx(y - hi.astype(F32))
    return _dot(hi, seg) + _dot(lo, seg)


def _params(n_parallel_axes, n_axes):
    sem = ("parallel",) * n_parallel_axes + ("arbitrary",) * (n_axes - n_parallel_axes)
    return pltpu.CompilerParams(dimension_semantics=sem, vmem_limit_bytes=VMEM_LIMIT)


def _full(shape):
    nd = len(shape)
    return pl.BlockSpec(shape, lambda *_: (0,) * nd)


def _mixer_in_kernel(x_ref, pos_ref, g_ref, win_ref, vseg_ref, vgain_ref, qlg_ref, wuq_ref, kvlg_ref,
                     wuk_ref, wuv_ref, eplace_ref, rot_ref, ones_ref, qg_ref, kg_ref, invf_ref,
                     ug_ref, vn_ref, q_ref, k_ref, v_ref):
    h = _mx(_rms(x_ref[...], g_ref[...]))
    z = _dot(h, win_ref[...])
    u = z[:, :GM_WIDTH]
    v = z[:, GM_WIDTH:2 * GM_WIDTH]
    q_lat = z[:, 2 * GM_WIDTH:2 * GM_WIDTH + MLA_Q_RANK]
    kv_lat = z[:, 2 * GM_WIDTH + MLA_Q_RANK:2 * GM_WIDTH + MLA_Q_RANK + MLA_KV_RANK]
    k_pe = z[:, IN_WIDTH_PAD - LANES:]

    ug_ref[...] = _gelu(u).astype(ug_ref.dtype)
    gv = _gelu(v)
    vseg = vseg_ref[...]
    for s in range(GM_WIDTH // LANES):
        sl = slice(s * LANES, (s + 1) * LANES)
        y = gv[:, sl]
        ms = _seg_sum(y * y, vseg) * (1.0 / GM_HEAD_DIM)
        vn_ref[:, sl] = (y * lax.rsqrt(ms + EPS) * vgain_ref[:, sl]).astype(vn_ref.dtype)

    ang = pos_ref[...].astype(F32) * invf_ref[...]
    cos = jnp.cos(ang)
    sin = jnp.sin(ang)
    rot = rot_ref[...]
    ones = ones_ref[...]

    def qk_norm_rope(raw, gain, scale):
        ms = _seg_sum(raw * raw, ones) * (1.0 / MLA_QK_DIM)
        y = raw * lax.rsqrt(ms + EPS) * gain
        return (y * cos + _dot(_mx(y), rot) * sin) * scale

    qn = _mx(_rms(q_lat, qlg_ref[...]))
    q_raw = _dot(qn, wuq_ref[...])
    kvn = _mx(_rms(kv_lat, kvlg_ref[...]))
    k_raw = _dot(kvn, wuk_ref[...]) + _dot(_mx(k_pe), eplace_ref[...])
    v_ref[...] = _dot(kvn, wuv_ref[...]).astype(v_ref.dtype)
    q_scale = MLA_QK_DIM ** -0.5
    for hd in range(MLA_HEADS):
        sl = slice(hd * HEAD_PAD, (hd + 1) * HEAD_PAD)
        q_ref[:, sl] = qk_norm_rope(q_raw[:, sl], qg_ref[...], q_scale).astype(q_ref.dtype)
        k_ref[:, sl] = qk_norm_rope(k_raw[:, sl], kg_ref[...], 1.0).astype(k_ref.dtype)


def _mixer_in(x2, pos2, g, w_in, gm_v_gain, q_lat_gain, w_uq, kv_lat_gain, w_ukv, q_gain, k_gain, tile):
    n = x2.shape[0]
    win = _mx(jnp.pad(w_in, ((0, 0), (0, IN_WIDTH_PAD - IN_WIDTH))))
    lane = jnp.arange(LANES)
    vseg = _mx((lane[:, None] // GM_HEAD_DIM == lane[None, :] // GM_HEAD_DIM).astype(F32))
    ones = _mx(jnp.ones((LANES, LANES), F32))
    vgain = gm_v_gain.reshape(1, GM_WIDTH)
    wuq = w_uq.reshape(MLA_Q_RANK, MLA_HEADS, MLA_QK_DIM)
    wuq = _mx(jnp.pad(wuq, ((0, 0), (0, 0), (0, HEAD_PAD - MLA_QK_DIM))).reshape(MLA_Q_RANK, MLA_HEADS * HEAD_PAD))
    wukv = w_ukv.reshape(MLA_KV_RANK, MLA_HEADS, MLA_NOPE_DIM + MLA_V_DIM)
    wuk = jnp.pad(wukv[:, :, :MLA_NOPE_DIM], ((0, 0), (0, 0), (0, HEAD_PAD - MLA_NOPE_DIM)))
    wuk = _mx(wuk.reshape(MLA_KV_RANK, MLA_HEADS * HEAD_PAD))
    wuv = _mx(wukv[:, :, MLA_NOPE_DIM:].reshape(MLA_KV_RANK, MLA_WIDTH))
    r = jnp.arange(MLA_ROPE_DIM)
    eplace = jnp.zeros((LANES, MLA_HEADS, HEAD_PAD), F32).at[r, :, MLA_NOPE_DIM + r].set(1.0)
    eplace = _mx(eplace.reshape(LANES, MLA_HEADS * HEAD_PAD))
    half = MLA_ROPE_DIM // 2
    j = jnp.arange(half)
    rot = jnp.zeros((HEAD_PAD, HEAD_PAD), F32)
    rot = rot.at[MLA_NOPE_DIM + half + j, MLA_NOPE_DIM + j].set(-1.0).at[MLA_NOPE_DIM + j, MLA_NOPE_DIM + half + j].set(1.0)
    rot = _mx(rot)
    inv_freq = ROPE_THETA ** (-jnp.arange(half, dtype=F32) / half)
    invf = jnp.zeros((1, HEAD_PAD), F32).at[0, MLA_NOPE_DIM + j].set(inv_freq).at[0, MLA_NOPE_DIM + half + j].set(inv_freq)
    qg = jnp.pad(q_gain, (0, HEAD_PAD - MLA_QK_DIM)).reshape(1, HEAD_PAD)
    kg = jnp.pad(k_gain, (0, HEAD_PAD - MLA_QK_DIM)).reshape(1, HEAD_PAD)

    row = lambda w: pl.BlockSpec((tile, w), lambda i: (i, 0))
    consts = (g.reshape(1, D_MODEL), win, vseg, vgain, q_lat_gain.reshape(1, -1), wuq, kv_lat_gain.reshape(1, -1),
              wuk, wuv, eplace, rot, ones, qg, kg, invf)
    out_dt = MXU_DTYPE
    return pl.pallas_call(
        _mixer_in_kernel,
        grid=(n // tile,),
        in_specs=[row(D_MODEL), row(1)] + [_full(c.shape) for c in consts],
        out_specs=[row(GM_WIDTH), row(GM_WIDTH), row(MLA_HEADS * HEAD_PAD), row(MLA_HEADS * HEAD_PAD), row(MLA_WIDTH)],
        out_shape=[jax.ShapeDtypeStruct((n, GM_WIDTH), out_dt), jax.ShapeDtypeStruct((n, GM_WIDTH), out_dt),
                   jax.ShapeDtypeStruct((n, MLA_HEADS * HEAD_PAD), out_dt),
                   jax.ShapeDtypeStruct((n, MLA_HEADS * HEAD_PAD), out_dt),
                   jax.ShapeDtypeStruct((n, MLA_WIDTH), out_dt)],
        compiler_params=_params(1, 1),
        name="mixer_in",
    )(x2, pos2, *consts)


def _attention_kernel(q_ref, k_ref, v_ref, o_ref, *, tq):
    qi = pl.program_id(2)
    lane = lax.broadcasted_iota(jnp.int32, (tq, LANES), 1)
    outs = []
    for hh in range(2):
        hs = slice(hh * HEAD_PAD, (hh + 1) * HEAD_PAD)
        q = q_ref[:, hs]

        def scores(j):
            kj = k_ref[pl.ds(pl.multiple_of(j * tq, tq), tq), hs]
            return _dot_nt(q, kj)

        def update(carry, s, j):
            m, l, acc = carry
            m_new = jnp.maximum(m, jnp.max(s, axis=-1, keepdims=True))
            alpha = jnp.exp(m - m_new)
            p = jnp.exp(s - m_new)
            vj = v_ref[pl.ds(pl.multiple_of(j * tq, tq), tq), :]
            acc = acc * alpha + _dot(_mx(p), vj)
            return m_new, l * alpha + jnp.sum(p, axis=-1, keepdims=True), acc

        def body(j, carry):
            return update(carry, scores(j), j)

        init = (jnp.full((tq, 1), NEG, F32), jnp.zeros((tq, 1), F32), jnp.zeros((tq, LANES), F32))
        carry = lax.fori_loop(0, qi, body, init)
        r = lax.broadcasted_iota(jnp.int32, (tq, tq), 0)
        c = lax.broadcasted_iota(jnp.int32, (tq, tq), 1)
        m, l, acc = update(carry, jnp.where(r >= c, scores(qi), NEG), qi)
        outs.append(acc / l)
    o_ref[...] = jnp.where(lane < MLA_V_DIM, outs[0], outs[1]).astype(o_ref.dtype)


def _attention(q, k, v, batch, seq, tq):
    n = batch * seq
    nq = seq // tq
    return pl.pallas_call(
        functools.partial(_attention_kernel, tq=tq),
        grid=(batch, MLA_HEADS // 2, nq),
        in_specs=[pl.BlockSpec((tq, 2 * HEAD_PAD), lambda b, hp, i: (b * nq + i, hp)),
                  pl.BlockSpec((seq, 2 * HEAD_PAD), lambda b, hp, i: (b, hp)),
                  pl.BlockSpec((seq, 2 * MLA_V_DIM), lambda b, hp, i: (b, hp))],
        out_specs=pl.BlockSpec((tq, 2 * MLA_V_DIM), lambda b, hp, i: (b * nq + i, hp)),
        out_shape=jax.ShapeDtypeStruct((n, MLA_WIDTH), MXU_DTYPE),
        compiler_params=_params(3, 3),
        name="attention",
    )(q, k, v)


def _mixer_out_kernel(ug_ref, vn_ref, o_ref, x_ref, ws_ref, bias_ref, woa_ref, wob_ref, out_ref, a_ref, *, tile):
    lane = lax.broadcasted_iota(jnp.int32, (GM_CHUNK, LANES), 1)
    left = lane < GM_HEAD_DIM
    for c in range(tile // GM_CHUNK):
        rows = slice(c * GM_CHUNK, (c + 1) * GM_CHUNK)
        for p in range(GM_WIDTH // LANES):
            sl = slice(p * LANES, (p + 1) * LANES)
            vs = vn_ref[rows, sl]
            zero = jnp.zeros_like(vs)
            rhs = jnp.concatenate([jnp.where(left, vs, zero), jnp.where(left, zero, vs)], axis=0)
            mixed = _dot(ws_ref[p], rhs) + bias_ref[:, sl]
            a_ref[rows, sl] = (ug_ref[rows, sl].astype(F32) * mixed).astype(a_ref.dtype)
    out_ref[...] = x_ref[...] + _dot(a_ref[...], woa_ref[...]) + _dot(o_ref[...], wob_ref[...])


def _mixer_out(ug, vn, o, x2, gm_w_s, gm_b_s, w_out, tile):
    n = x2.shape[0]
    causal = jnp.tril(jnp.ones((GM_CHUNK, GM_CHUNK), bool))
    w = jnp.where(causal[None], gm_w_s, 0.0)
    ws = _mx(w.reshape(GM_HEADS // 2, 2, GM_CHUNK, GM_CHUNK).transpose(0, 2, 1, 3).reshape(GM_HEADS // 2, GM_CHUNK, 2 * GM_CHUNK))
    bias = jnp.repeat(gm_b_s.T, GM_HEAD_DIM, axis=1)
    woa = _mx(w_out[:GM_WIDTH])
    wob = _mx(w_out[GM_WIDTH:])
    row = lambda w_: pl.BlockSpec((tile, w_), lambda i: (i, 0))
    return pl.pallas_call(
        functools.partial(_mixer_out_kernel, tile=tile),
        grid=(n // tile,),
        in_specs=[row(GM_WIDTH), row(GM_WIDTH), row(MLA_WIDTH), row(D_MODEL),
                  _full(ws.shape), _full(bias.shape), _full(woa.shape), _full(wob.shape)],
        out_specs=row(D_MODEL),
        out_shape=jax.ShapeDtypeStruct((n, D_MODEL), F32),
        scratch_shapes=[pltpu.VMEM((tile, GM_WIDTH), MXU_DTYPE)],
        compiler_params=_params(1, 1),
        name="mixer_out",
    )(ug, vn, o, x2, ws, bias, woa, wob)


def _conformer_kernel(x_ref, g_ref, w1_ref, b1_ref, wdw_ref, bdw_ref, ng_ref, w2_ref, b2_ref, out_ref, zext_ref, *, tile):
    @pl.when(pl.program_id(1) == 0)
    def _():
        zext_ref[0:CONV_HALO, :] = jnp.zeros((CONV_HALO, CONV_CH), F32)

    x = x_ref[...]
    h = _mx(_rms(x, g_ref[...]))
    z = _dot(h, w1_ref[...]) + b1_ref[...]
    glu = z[:, :CONV_CH] * jax.nn.sigmoid(z[:, CONV_CH:])
    zext_ref[CONV_HALO:CONV_HALO + tile, :] = glu
    first = CONV_HALO - (CONV_WIDTH - 1)
    cols = []
    for cb in range(CONV_CH // LANES):
        sl = slice(cb * LANES, (cb + 1) * LANES)
        acc = jnp.zeros((tile, LANES), F32)
        for k in range(CONV_WIDTH):
            acc = acc + zext_ref[first + k:first + k + tile, sl] * wdw_ref[k:k + 1, sl]
        cols.append(acc)
    y = jnp.concatenate(cols, axis=1) + bdw_ref[...]
    zext_ref[0:CONV_HALO, :] = zext_ref[tile:tile + CONV_HALO, :]
    yn = _rms(y, ng_ref[...])
    s = yn * jax.nn.sigmoid(yn)
    out_ref[...] = x + _dot(_mx(s), w2_ref[...]) + b2_ref[...]


def _conformer(x2, g, w_pw1, b_pw1, w_dw, b_dw, norm_gain, w_pw2, b_pw2, batch, seq, tile):
    n = x2.shape[0]
    nt = seq // tile
    consts = (g.reshape(1, -1), _mx(w_pw1), b_pw1.reshape(1, -1), w_dw, b_dw.reshape(1, -1),
              norm_gain.reshape(1, -1), _mx(w_pw2), b_pw2.reshape(1, -1))
    row = pl.BlockSpec((tile, D_MODEL), lambda b, i: (b * nt + i, 0))
    return pl.pallas_call(
        functools.partial(_conformer_kernel, tile=tile),
        grid=(batch, nt),
        in_specs=[row] + [_full(c.shape) for c in consts],
        out_specs=row,
        out_shape=jax.ShapeDtypeStruct((n, D_MODEL), F32),
        scratch_shapes=[pltpu.VMEM((CONV_HALO + tile, CONV_CH), F32)],
        compiler_params=_params(0, 2),
        name="conformer",
    )(x2, *consts)


def _oddeven_merge(lo, hi, r):
    step = r * 2
    if step < hi - lo:
        yield from _oddeven_merge(lo, hi, step)
        yield from _oddeven_merge(lo + r, hi, step)
        yield from [(i, i + r) for i in range(lo + r, hi - r, step)]
    else:
        yield (lo, lo + r)


def _oddeven_sort(lo, hi):
    if hi - lo >= 1:
        mid = lo + (hi - lo) // 2
        yield from _oddeven_sort(lo, mid)
        yield from _oddeven_sort(mid + 1, hi)
        yield from _oddeven_merge(lo, hi, 1)


SORT16 = tuple(_oddeven_sort(0, PEER_TOPK - 1))
BITONIC16 = tuple((i, i + d) for d in (8, 4, 2, 1) for i in range(PEER_TOPK) if not i & d)
SUBLANES = 8


def _compare_exchange(b, i, j):
    b[i], b[j] = jnp.maximum(b[i], b[j]), jnp.minimum(b[i], b[j])


def _top16(blocks):
    b = list(blocks)
    for i, j in SORT16:
        _compare_exchange(b, i, j)
    for shift in (4, 2, 1):
        b = [jnp.maximum(b[i], pltpu.roll(b[PEER_TOPK - 1 - i], shift, axis=0)) for i in range(PEER_TOPK)]
        for i, j in BITONIC16:
            _compare_exchange(b, i, j)
    return b


def _all_sublanes(x, op):
    for shift in (4, 2, 1):
        x = op(x, pltpu.roll(x, shift, axis=0))
    return x


def _peer_route_kernel(x_ref, g_ref, wqt_ref, keys_ref, ht_ref, r2_ref, e2_ref, n_ref, coef_ref, *, tile):
    h = _rms(x_ref[...], g_ref[...])
    ht = _mx(h.T)
    ht_ref[...] = ht
    qt = _dot(wqt_ref[...], ht)
    row = lax.broadcasted_iota(jnp.int32, (SUBLANES, LANES), 0)
    nblk = PEER_N_KEYS // SUBLANES

    def stack(blocks8):
        out = blocks8[SUBLANES - 1]
        for r in range(SUBLANES - 2, -1, -1):
            out = jnp.where(row == r, blocks8[r], out)
        return out

    gate_dt = MXU_DTYPE
    pack = 4 // jnp.dtype(gate_dt).itemsize

    def store(ref, hd, ls, blocks):
        for i in range(0, nblk, pack):
            v = jnp.concatenate(blocks[i:i + pack], axis=0).astype(gate_dt)
            ref[hd, i // pack * SUBLANES:(i // pack + 1) * SUBLANES, ls] = v if pack == 1 else pltpu.bitcast(v, F32)

    def store_words(ref, hd, ls, blocks):
        for i in range(nblk):
            v = blocks[i]
            if pack == 2:
                bits = pltpu.bitcast(v.astype(gate_dt).astype(F32), jnp.uint32)
                v = pltpu.bitcast(bits | (bits >> 16), F32)
            ref[hd, i * SUBLANES:(i + 1) * SUBLANES, ls] = v

    for hd in range(PEER_HEADS):
        base = hd * 2 * PEER_HALF
        s1 = _dot(keys_ref[0, hd], _mx(qt[base:base + PEER_HALF]))
        s2 = _dot(keys_ref[1, hd], _mx(qt[base + PEER_HALF:base + 2 * PEER_HALF]))
        for lc in range(tile // LANES):
            ls = slice(lc * LANES, (lc + 1) * LANES)
            a1 = [s1[i * SUBLANES:(i + 1) * SUBLANES, ls] for i in range(nblk)]
            a2 = [s2[i * SUBLANES:(i + 1) * SUBLANES, ls] for i in range(nblk)]
            b1 = _top16(a1)
            b2 = _top16(a2)
            c1lo, c1hi = stack(b1[:SUBLANES]), stack(b1[SUBLANES:])
            c2lo, c2hi = stack(b2[:SUBLANES]), stack(b2[SUBLANES:])
            cand = []
            for a in range(4):
                cand += [b1[a] + c2lo, b1[a] + c2hi]
            c1lo_rest = jnp.where(row < 4, NEG, c1lo)
            for b in range(4):
                cand += [b2[b] + c1lo_rest, b2[b] + c1hi]
            top = _top16(cand)
            tau = top[PEER_TOPK - 1]
            zsum = jnp.ones_like(tau)
            for r in range(1, PEER_TOPK):
                zsum = zsum + jnp.exp(top[r] - top[0])
            inv_z = 1.0 / zsum
            theta = []
            for b in range(SUBLANES):
                v = jnp.minimum(jnp.where(c1lo + b2[b] >= tau, c1lo, -NEG), jnp.where(c1hi + b2[b] >= tau, c1hi, -NEG))
                theta.append(_all_sublanes(v, jnp.minimum))
            n_hi = _all_sublanes(jnp.where(b1[0] + c2hi >= tau, 1.0, 0.0), jnp.add)
            n_blocks, coef_blocks, r2_blocks, e2_blocks = [], [], [], []
            for i in range(nblk):
                cnt = jnp.zeros((SUBLANES, LANES), F32)
                for b in range(SUBLANES):
                    cnt = jnp.where(a1[i] >= theta[b], float(b + 1), cnt)
                n_blocks.append(cnt + jnp.where(a1[i] == b1[0], n_hi, 0.0))
                coef_blocks.append(jnp.exp(a1[i] - b1[0]) * inv_z)
                rank = jnp.full((SUBLANES, LANES), float(PEER_TOPK), F32)
                for b in range(PEER_TOPK - 1, -1, -1):
                    rank = jnp.where(a2[i] >= b2[b], float(b), rank)
                r2_blocks.append(rank)
                e2_blocks.append(jnp.exp(a2[i] - b2[0]))
            store_words(n_ref, hd, ls, n_blocks)
            store_words(coef_ref, hd, ls, coef_blocks)
            store(r2_ref, hd, ls, r2_blocks)
            store(e2_ref, hd, ls, e2_blocks)


def _peer_route(x2, g, w_q, sub_keys, tile):
    n = x2.shape[0]
    wqt = _mx(w_q.T)
    keys = _mx(sub_keys)
    pack = 4 // jnp.dtype(MXU_DTYPE).itemsize
    tok = pl.BlockSpec((PEER_HEADS, PEER_N_KEYS, tile), lambda i: (0, 0, i))
    tokp = pl.BlockSpec((PEER_HEADS, PEER_N_KEYS // pack, tile), lambda i: (0, 0, i))
    packed = jax.ShapeDtypeStruct((PEER_HEADS, PEER_N_KEYS // pack, n), F32)
    words = jax.ShapeDtypeStruct((PEER_HEADS, PEER_N_KEYS, n), F32)
    return pl.pallas_call(
        functools.partial(_peer_route_kernel, tile=tile),
        grid=(n // tile,),
        in_specs=[pl.BlockSpec((tile, D_MODEL), lambda i: (i, 0)), _full((1, D_MODEL)), _full(wqt.shape), _full(keys.shape)],
        out_specs=[pl.BlockSpec((D_MODEL, tile), lambda i: (0, i)), tokp, tokp, tok, tok],
        out_shape=[jax.ShapeDtypeStruct((D_MODEL, n), MXU_DTYPE), packed, packed, words, words],
        compiler_params=_params(1, 1),
        name="peer_route",
    )(x2, g.reshape(1, D_MODEL), wqt, keys)


def _peer_expert_kernel(ht_ref, u_ref, vt_ref, n_ref, coef_ref, r2_ref, e2_ref, x_ref, out_ref,
                        acc_ref, g_ref, *, tile, ib):
    step = pl.program_id(1)

    @pl.when(step == 0)
    def _():
        acc_ref[...] = jnp.zeros_like(acc_ref)

    ht = ht_ref[...]
    gate_dt = g_ref.dtype
    pack = 4 // jnp.dtype(gate_dt).itemsize
    rows_per_vreg = SUBLANES * pack
    shape3 = (PEER_N_KEYS // rows_per_vreg, rows_per_vreg, LANES)
    zero = jnp.zeros(shape3, gate_dt)

    def row_vreg(ref, hd, k, ls):
        word = jnp.broadcast_to(ref[hd, k:k + 1, ls], (SUBLANES, LANES))
        return word if pack == 1 else pltpu.bitcast(word, gate_dt)

    def table(ref, hd, ls):
        words = ref[hd, :, ls]
        return (words if pack == 1 else pltpu.bitcast(words, gate_dt)).reshape(shape3)

    for k in range(ib):
        rows = slice(k * PEER_N_KEYS, (k + 1) * PEER_N_KEYS)
        act = _gelu(_dot(u_ref[rows, :], ht))
        for lc in range(tile // LANES):
            ls = slice(lc * LANES, (lc + 1) * LANES)
            w = zero
            for hd in range(PEER_HEADS):
                hit = table(r2_ref, hd, ls) < row_vreg(n_ref, hd, k, ls)[None]
                w = w + jnp.where(hit, table(e2_ref, hd, ls), zero) * row_vreg(coef_ref, hd, k, ls)[None]
            g_ref[rows, ls] = (w * act[:, ls].astype(gate_dt).reshape(shape3)).reshape(PEER_N_KEYS, LANES)
    acc_ref[...] += _dot(vt_ref[...], g_ref[...])

    @pl.when(step == pl.num_programs(1) - 1)
    def _():
        out_ref[...] = x_ref[...] + acc_ref[...].T


def _peer_experts(x2, ht, r2, e2, cnt, coef, u, vt, tile, ib):
    n = x2.shape[0]
    blk = ib * PEER_N_KEYS
    small = pl.BlockSpec((PEER_HEADS, ib, tile), lambda i, j: (0, j, i))
    big = pl.BlockSpec((PEER_HEADS, r2.shape[1], tile), lambda i, j: (0, 0, i))
    return pl.pallas_call(
        functools.partial(_peer_expert_kernel, tile=tile, ib=ib),
        grid=(n // tile, PEER_N_KEYS // ib),
        in_specs=[pl.BlockSpec((D_MODEL, tile), lambda i, j: (0, i)),
                  pl.BlockSpec((blk, D_MODEL), lambda i, j: (j, 0)),
                  pl.BlockSpec((D_MODEL, blk), lambda i, j: (0, j)),
                  small, small, big, big,
                  pl.BlockSpec((tile, D_MODEL), lambda i, j: (i, 0))],
        out_specs=pl.BlockSpec((tile, D_MODEL), lambda i, j: (i, 0)),
        out_shape=jax.ShapeDtypeStruct((n, D_MODEL), F32),
        scratch_shapes=[pltpu.VMEM((D_MODEL, tile), F32), pltpu.VMEM((blk, tile), MXU_DTYPE)],
        compiler_params=_params(1, 2),
        name="peer_experts",
    )(ht, u, vt, cnt, coef, r2, e2, x2)


def _peer(x2, g, w_q, sub_keys, expert_u, expert_v, route_tile, tile, ib):
    ht, r2, e2, cnt, coef = _peer_route(x2, g, w_q, sub_keys, route_tile)
    return _peer_experts(x2, ht, r2, e2, cnt, coef, _mx(expert_u), _mx(expert_v.T), tile, ib)


def _tile(n, want):
    t = min(n, want)
    assert n % t == 0, (n, t)
    return t


def kernel(x, positions, mix_norm_even, w_in, gm_v_gain, gm_w_s, gm_b_s, mla_q_lat_gain, mla_w_uq, mla_kv_lat_gain, mla_w_ukv, mla_q_gain, mla_k_gain, w_out, mix_norm_odd, conv_w_pw1, conv_b_pw1, conv_w_dw, conv_b_dw, conv_norm_gain, conv_w_pw2, conv_b_pw2, ffn_norm, peer_w_q, peer_sub_keys, peer_u, peer_v):
    batch, seq, d = x.shape
    n = batch * seq
    depth = ffn_norm.shape[0]
    x2 = x.reshape(n, d)
    pos2 = positions.reshape(n, 1)
    row_tile = _tile(n, 512)
    for layer in range(depth):
        i = layer // 2
        if layer % 2 == 0:
            ug, vn, q, k, v = _mixer_in(x2, pos2, mix_norm_even[i], w_in[i], gm_v_gain[i], mla_q_lat_gain[i],
                                        mla_w_uq[i], mla_kv_lat_gain[i], mla_w_ukv[i], mla_q_gain[i],
                                        mla_k_gain[i], row_tile)
            o = _attention(q, k, v, batch, seq, _tile(seq, 256))
            x2 = _mixer_out(ug, vn, o, x2, gm_w_s[i], gm_b_s[i], w_out[i], row_tile)
        else:
            x2 = _conformer(x2, mix_norm_odd[i], conv_w_pw1[i], conv_b_pw1[i], conv_w_dw[i], conv_b_dw[i],
                            conv_norm_gain[i], conv_w_pw2[i], conv_b_pw2[i], batch, seq, _tile(seq, 256))
        x2 = _peer(x2, ffn_norm[layer], peer_w_q[layer], peer_sub_keys[layer], peer_u[layer], peer_v[layer],
                   _tile(n, 256), row_tile, 16)
    return x2.reshape(batch, seq, d)
```

```python
import functools
import math

import jax
import jax.numpy as jnp
from jax import lax
from jax.experimental import pallas as pl
from jax.experimental.pallas import tpu as pltpu

F32 = jnp.float32
MXU_DTYPE = jnp.bfloat16

D_MODEL = 1024
EPS = 1e-6
NEG = -1e30

GM_HEADS = 8
GM_HEAD_DIM = 64
GM_WIDTH = 512
GM_CHUNK = 128

MLA_HEADS = 8
MLA_Q_RANK = 384
MLA_KV_RANK = 256
MLA_NOPE_DIM = 64
MLA_ROPE_DIM = 32
MLA_QK_DIM = 96
MLA_V_DIM = 64
MLA_WIDTH = 512
ROPE_THETA = 10000.0
HEAD_PAD = 128

IN_WIDTH = 1696
IN_WIDTH_PAD = 1792

CONV_CH = 1024
CONV_WIDTH = 31
CONV_HALO = 32

PEER_HEADS = 8
PEER_N_KEYS = 128
PEER_HALF = 128
PEER_TOPK = 16

LANES = 128
VMEM_LIMIT = 56 * 1024 * 1024


def _dot(a, b):
    return jnp.dot(a, b, preferred_element_type=F32)


def _dot_nt(a, b):
    return lax.dot_general(a, b, (((1,), (1,)), ((), ())), preferred_element_type=F32)


def _mx(a):
    return a.astype(MXU_DTYPE)


def _gelu(x):
    c = math.sqrt(2.0 / math.pi)
    return x * (0.5 + 0.5 * jnp.tanh(x * (c + (c * 0.044715) * (x * x))))


def _rms(x, g):
    ms = jnp.mean(x * x, axis=-1, keepdims=True)
    return x * lax.rsqrt(ms + EPS) * g


def _seg_sum(y, seg):
    hi = _mx(y)
    lo = _mx(y - hi.astype(F32))
    return _dot(hi, seg) + _dot(lo, seg)


def _params(n_parallel_axes, n_axes):
    sem = ("parallel",) * n_parallel_axes + ("arbitrary",) * (n_axes - n_parallel_axes)
    return pltpu.CompilerParams(dimension_semantics=sem, vmem_limit_bytes=VMEM_LIMIT)


def _full(shape):
    nd = len(shape)
    return pl.BlockSpec(shape, lambda *_: (0,) * nd)


def _mixer_in_kernel(x_ref, pos_ref, g_ref, win_ref, vseg_ref, vgain_ref, qlg_ref, wuq_ref, kvlg_ref,
                     wuk_ref, wuv_ref, eplace_ref, rot_ref, ones_ref, qg_ref, kg_ref, invf_ref,
                     ug_ref, vn_ref, q_ref, k_ref, v_ref):
    h = _mx(_rms(x_ref[...], g_ref[...]))
    z = _dot(h, win_ref[...])
    u = z[:, :GM_WIDTH]
    v = z[:, GM_WIDTH:2 * GM_WIDTH]
    q_lat = z[:, 2 * GM_WIDTH:2 * GM_WIDTH + MLA_Q_RANK]
    kv_lat = z[:, 2 * GM_WIDTH + MLA_Q_RANK:2 * GM_WIDTH + MLA_Q_RANK + MLA_KV_RANK]
    k_pe = z[:, IN_WIDTH_PAD - LANES:]

    ug_ref[...] = _gelu(u).astype(ug_ref.dtype)
    gv = _gelu(v)
    vseg = vseg_ref[...]
    for s in range(GM_WIDTH // LANES):
        sl = slice(s * LANES, (s + 1) * LANES)
        y = gv[:, sl]
        ms = _seg_sum(y * y, vseg) * (1.0 / GM_HEAD_DIM)
        vn_ref[:, sl] = (y * lax.rsqrt(ms + EPS) * vgain_ref[:, sl]).astype(vn_ref.dtype)

    ang = pos_ref[...].astype(F32) * invf_ref[...]
    cos = jnp.cos(ang)
    sin = jnp.sin(ang)
    rot = rot_ref[...]
    ones = ones_ref[...]

    def qk_norm_rope(raw, gain, scale):
        ms = _seg_sum(raw * raw, ones) * (1.0 / MLA_QK_DIM)
        y = raw * lax.rsqrt(ms + EPS) * gain
        return (y * cos + _dot(_mx(y), rot) * sin) * scale

    qn = _mx(_rms(q_lat, qlg_ref[...]))
    q_raw = _dot(qn, wuq_ref[...])
    kvn = _mx(_rms(kv_lat, kvlg_ref[...]))
    k_raw = _dot(kvn, wuk_ref[...]) + _dot(_mx(k_pe), eplace_ref[...])
    v_ref[...] = _dot(kvn, wuv_ref[...]).astype(v_ref.dtype)
    q_scale = MLA_QK_DIM ** -0.5
    for hd in range(MLA_HEADS):
        sl = slice(hd * HEAD_PAD, (hd + 1) * HEAD_PAD)
        q_ref[:, sl] = qk_norm_rope(q_raw[:, sl], qg_ref[...], q_scale).astype(q_ref.dtype)
        k_ref[:, sl] = qk_norm_rope(k_raw[:, sl], kg_ref[...], 1.0).astype(k_ref.dtype)


def _mixer_in(x2, pos2, g, w_in, gm_v_gain, q_lat_gain, w_uq, kv_lat_gain, w_ukv, q_gain, k_gain, tile):
    n = x2.shape[0]
    win = _mx(jnp.pad(w_in, ((0, 0), (0, IN_WIDTH_PAD - IN_WIDTH))))
    lane = jnp.arange(LANES)
    vseg = _mx((lane[:, None] // GM_HEAD_DIM == lane[None, :] // GM_HEAD_DIM).astype(F32))
    ones = _mx(jnp.ones((LANES, LANES), F32))
    vgain = gm_v_gain.reshape(1, GM_WIDTH)
    wuq = w_uq.reshape(MLA_Q_RANK, MLA_HEADS, MLA_QK_DIM)
    wuq = _mx(jnp.pad(wuq, ((0, 0), (0, 0), (0, HEAD_PAD - MLA_QK_DIM))).reshape(MLA_Q_RANK, MLA_HEADS * HEAD_PAD))
    wukv = w_ukv.reshape(MLA_KV_RANK, MLA_HEADS, MLA_NOPE_DIM + MLA_V_DIM)
    wuk = jnp.pad(wukv[:, :, :MLA_NOPE_DIM], ((0, 0), (0, 0), (0, HEAD_PAD - MLA_NOPE_DIM)))
    wuk = _mx(wuk.reshape(MLA_KV_RANK, MLA_HEADS * HEAD_PAD))
    wuv = _mx(wukv[:, :, MLA_NOPE_DIM:].reshape(MLA_KV_RANK, MLA_WIDTH))
    r = jnp.arange(MLA_ROPE_DIM)
    eplace = jnp.zeros((LANES, MLA_HEADS, HEAD_PAD), F32).at[r, :, MLA_NOPE_DIM + r].set(1.0)
    eplace = _mx(eplace.reshape(LANES, MLA_HEADS * HEAD_PAD))
    half = MLA_ROPE_DIM // 2
    j = jnp.arange(half)
    rot = jnp.zeros((HEAD_PAD, HEAD_PAD), F32)
    rot = rot.at[MLA_NOPE_DIM + half + j, MLA_NOPE_DIM + j].set(-1.0).at[MLA_NOPE_DIM + j, MLA_NOPE_DIM + half + j].set(1.0)
    rot = _mx(rot)
    inv_freq = ROPE_THETA ** (-jnp.arange(half, dtype=F32) / half)
    invf = jnp.zeros((1, HEAD_PAD), F32).at[0, MLA_NOPE_DIM + j].set(inv_freq).at[0, MLA_NOPE_DIM + half + j].set(inv_freq)
    qg = jnp.pad(q_gain, (0, HEAD_PAD - MLA_QK_DIM)).reshape(1, HEAD_PAD)
    kg = jnp.pad(k_gain, (0, HEAD_PAD - MLA_QK_DIM)).reshape(1, HEAD_PAD)

    row = lambda w: pl.BlockSpec((tile, w), lambda i: (i, 0))
    consts = (g.reshape(1, D_MODEL), win, vseg, vgain, q_lat_gain.reshape(1, -1), wuq, kv_lat_gain.reshape(1, -1),
              wuk, wuv, eplace, rot, ones, qg, kg, invf)
    out_dt = MXU_DTYPE
    return pl.pallas_call(
        _mixer_in_kernel,
        grid=(n // tile,),
        in_specs=[row(D_MODEL), row(1)] + [_full(c.shape) for c in consts],
        out_specs=[row(GM_WIDTH), row(GM_WIDTH), row(MLA_HEADS * HEAD_PAD), row(MLA_HEADS * HEAD_PAD), row(MLA_WIDTH)],
        out_shape=[jax.ShapeDtypeStruct((n, GM_WIDTH), out_dt), jax.ShapeDtypeStruct((n, GM_WIDTH), out_dt),
                   jax.ShapeDtypeStruct((n, MLA_HEADS * HEAD_PAD), out_dt),
                   jax.ShapeDtypeStruct((n, MLA_HEADS * HEAD_PAD), out_dt),
                   jax.ShapeDtypeStruct((n, MLA_WIDTH), out_dt)],
        compiler_params=_params(1, 1),
        name="mixer_in",
    )(x2, pos2, *consts)


def _attention_kernel(q_ref, k_ref, v_ref, o_ref, *, tq):
    qi = pl.program_id(2)
    lane = lax.broadcasted_iota(jnp.int32, (tq, LANES), 1)
    heads = (slice(0, HEAD_PAD), slice(HEAD_PAD, 2 * HEAD_PAD))
    qs = [q_ref[:, hs] for hs in heads]

    def scores(j, hh):
        kj = k_ref[pl.ds(pl.multiple_of(j * tq, tq), tq), heads[hh]]
        return _dot_nt(qs[hh], kj)

    def update(carry, s, vj):
        m, l, acc = carry
        m_new = jnp.maximum(m, jnp.max(s, axis=-1, keepdims=True))
        alpha = jnp.exp(m - m_new)
        p = jnp.exp(s - m_new)
        acc = acc * alpha + _dot(_mx(p), vj)
        return m_new, l * alpha + jnp.sum(p, axis=-1, keepdims=True), acc

    def body(j, carry):
        vj = v_ref[pl.ds(pl.multiple_of(j * tq, tq), tq), :]
        return tuple(update(carry[hh], scores(j, hh), vj) for hh in range(2))

    init = (jnp.full((tq, 1), NEG, F32), jnp.zeros((tq, 1), F32), jnp.zeros((tq, LANES), F32))
    carry = lax.fori_loop(0, qi, body, (init, init))
    r = lax.broadcasted_iota(jnp.int32, (tq, tq), 0)
    c = lax.broadcasted_iota(jnp.int32, (tq, tq), 1)
    vd = v_ref[pl.ds(pl.multiple_of(qi * tq, tq), tq), :]
    outs = []
    for hh in range(2):
        m, l, acc = update(carry[hh], jnp.where(r >= c, scores(qi, hh), NEG), vd)
        outs.append(acc / l)
    o_ref[...] = jnp.where(lane < MLA_V_DIM, outs[0], outs[1]).astype(o_ref.dtype)


def _attention(q, k, v, batch, seq, tq):
    n = batch * seq
    nq = seq // tq
    return pl.pallas_call(
        functools.partial(_attention_kernel, tq=tq),
        grid=(batch, MLA_HEADS // 2, nq),
        in_specs=[pl.BlockSpec((tq, 2 * HEAD_PAD), lambda b, hp, i: (b * nq + i, hp)),
                  pl.BlockSpec((seq, 2 * HEAD_PAD), lambda b, hp, i: (b, hp)),
                  pl.BlockSpec((seq, 2 * MLA_V_DIM), lambda b, hp, i: (b, hp))],
        out_specs=pl.BlockSpec((tq, 2 * MLA_V_DIM), lambda b, hp, i: (b * nq + i, hp)),
        out_shape=jax.ShapeDtypeStruct((n, MLA_WIDTH), MXU_DTYPE),
        compiler_params=_params(3, 3),
        name="attention",
    )(q, k, v)


def _mixer_out_kernel(ug_ref, vn_ref, o_ref, x_ref, ws_ref, bias_ref, woa_ref, wob_ref, out_ref, a_ref, *, tile):
    lane = lax.broadcasted_iota(jnp.int32, (GM_CHUNK, LANES), 1)
    left = lane < GM_HEAD_DIM
    for c in range(tile // GM_CHUNK):
        rows = slice(c * GM_CHUNK, (c + 1) * GM_CHUNK)
        for p in range(GM_WIDTH // LANES):
            sl = slice(p * LANES, (p + 1) * LANES)
            vs = vn_ref[rows, sl]
            zero = jnp.zeros_like(vs)
            rhs = jnp.concatenate([jnp.where(left, vs, zero), jnp.where(left, zero, vs)], axis=0)
            mixed = _dot(ws_ref[p], rhs) + bias_ref[:, sl]
            a_ref[rows, sl] = (ug_ref[rows, sl].astype(F32) * mixed).astype(a_ref.dtype)
    out_ref[...] = x_ref[...] + _dot(a_ref[...], woa_ref[...]) + _dot(o_ref[...], wob_ref[...])


def _mixer_out(ug, vn, o, x2, gm_w_s, gm_b_s, w_out, tile):
    n = x2.shape[0]
    causal = jnp.tril(jnp.ones((GM_CHUNK, GM_CHUNK), bool))
    w = jnp.where(causal[None], gm_w_s, 0.0)
    ws = _mx(w.reshape(GM_HEADS // 2, 2, GM_CHUNK, GM_CHUNK).transpose(0, 2, 1, 3).reshape(GM_HEADS // 2, GM_CHUNK, 2 * GM_CHUNK))
    bias = jnp.repeat(gm_b_s.T, GM_HEAD_DIM, axis=1)
    woa = _mx(w_out[:GM_WIDTH])
    wob = _mx(w_out[GM_WIDTH:])
    row = lambda w_: pl.BlockSpec((tile, w_), lambda i: (i, 0))
    return pl.pallas_call(
        functools.partial(_mixer_out_kernel, tile=tile),
        grid=(n // tile,),
        in_specs=[row(GM_WIDTH), row(GM_WIDTH), row(MLA_WIDTH), row(D_MODEL),
                  _full(ws.shape), _full(bias.shape), _full(woa.shape), _full(wob.shape)],
        out_specs=row(D_MODEL),
        out_shape=jax.ShapeDtypeStruct((n, D_MODEL), F32),
        scratch_shapes=[pltpu.VMEM((tile, GM_WIDTH), MXU_DTYPE)],
        compiler_params=_params(1, 1),
        name="mixer_out",
    )(ug, vn, o, x2, ws, bias, woa, wob)


def _conformer_kernel(x_ref, g_ref, w1_ref, b1_ref, wdw_ref, bdw_ref, ng_ref, w2_ref, b2_ref, out_ref, zext_ref, shift_ref, *, tile):
    @pl.when(pl.program_id(1) == 0)
    def _():
        zext_ref[0:CONV_HALO, :] = jnp.zeros((CONV_HALO, CONV_CH), F32)
        zext_ref[CONV_HALO + tile:, :] = jnp.zeros((SUBLANES, CONV_CH), F32)

    x = x_ref[...]
    h = _mx(_rms(x, g_ref[...]))
    z = _dot(h, w1_ref[...]) + b1_ref[...]
    glu = z[:, :CONV_CH] * jax.nn.sigmoid(z[:, CONV_CH:])
    zext_ref[CONV_HALO:CONV_HALO + tile, :] = glu
    first = CONV_HALO - (CONV_WIDTH - 1)
    cols = []
    for cb in range(CONV_CH // LANES):
        sl = slice(cb * LANES, (cb + 1) * LANES)
        acc = jnp.zeros((tile, LANES), F32)
        for r in range(SUBLANES):
            taps = [k for k in range(CONV_WIDTH) if (first + k) % SUBLANES == r]
            shift_ref[r] = zext_ref[r:r + CONV_HALO + tile, sl]
            for k in taps:
                off = first + k - r
                acc = acc + shift_ref[r, off:off + tile, :] * wdw_ref[k:k + 1, sl]
        cols.append(acc)
    y = jnp.concatenate(cols, axis=1) + bdw_ref[...]
    zext_ref[0:CONV_HALO, :] = zext_ref[tile:tile + CONV_HALO, :]
    yn = _rms(y, ng_ref[...])
    s = yn * jax.nn.sigmoid(yn)
    out_ref[...] = x + _dot(_mx(s), w2_ref[...]) + b2_ref[...]


def _conformer(x2, g, w_pw1, b_pw1, w_dw, b_dw, norm_gain, w_pw2, b_pw2, batch, seq, tile):
    n = x2.shape[0]
    nt = seq // tile
    consts = (g.reshape(1, -1), _mx(w_pw1), b_pw1.reshape(1, -1), w_dw, b_dw.reshape(1, -1),
              norm_gain.reshape(1, -1), _mx(w_pw2), b_pw2.reshape(1, -1))
    row = pl.BlockSpec((tile, D_MODEL), lambda b, i: (b * nt + i, 0))
    return pl.pallas_call(
        functools.partial(_conformer_kernel, tile=tile),
        grid=(batch, nt),
        in_specs=[row] + [_full(c.shape) for c in consts],
        out_specs=row,
        out_shape=jax.ShapeDtypeStruct((n, D_MODEL), F32),
        scratch_shapes=[pltpu.VMEM((CONV_HALO + tile + SUBLANES, CONV_CH), F32),
                        pltpu.VMEM((SUBLANES, CONV_HALO + tile, LANES), F32)],
        compiler_params=_params(0, 2),
        name="conformer",
    )(x2, *consts)


def _oddeven_merge(lo, hi, r):
    step = r * 2
    if step < hi - lo:
        yield from _oddeven_merge(lo, hi, step)
        yield from _oddeven_merge(lo + r, hi, step)
        yield from [(i, i + r) for i in range(lo + r, hi - r, step)]
    else:
        yield (lo, lo + r)


def _oddeven_sort(lo, hi):
    if hi - lo >= 1:
        mid = lo + (hi - lo) // 2
        yield from _oddeven_sort(lo, mid)
        yield from _oddeven_sort(mid + 1, hi)
        yield from _oddeven_merge(lo, hi, 1)


SORT16 = tuple(_oddeven_sort(0, PEER_TOPK - 1))
BITONIC16 = tuple((i, i + d) for d in (8, 4, 2, 1) for i in range(PEER_TOPK) if not i & d)
SUBLANES = 8
EXPERT_I1_GROUP = 2
EXPERT_I2_PARTS = 1
EXPERT_OUT_CHUNKS = 1


def _compare_exchange(b, i, j):
    b[i], b[j] = jnp.maximum(b[i], b[j]), jnp.minimum(b[i], b[j])


def _top16(blocks):
    b = list(blocks)
    for i, j in SORT16:
        _compare_exchange(b, i, j)
    for shift in (4, 2, 1):
        b = [jnp.maximum(b[i], pltpu.roll(b[PEER_TOPK - 1 - i], shift, axis=0)) for i in range(PEER_TOPK)]
        for i, j in BITONIC16:
            _compare_exchange(b, i, j)
    return b


def _all_sublanes(x, op):
    for shift in (4, 2, 1):
        x = op(x, pltpu.roll(x, shift, axis=0))
    return x


def _peer_route_kernel(x_ref, g_ref, wqt_ref, keys_ref, ht_ref, r2_ref, e2_ref, n_ref, coef_ref, *, tile):
    h = _rms(x_ref[...], g_ref[...])
    ht = _mx(h.T)
    ht_ref[...] = ht
    qt = _dot(wqt_ref[...], ht)
    row = lax.broadcasted_iota(jnp.int32, (SUBLANES, LANES), 0)
    nblk = PEER_N_KEYS // SUBLANES

    def stack(blocks8):
        out = blocks8[SUBLANES - 1]
        for r in range(SUBLANES - 2, -1, -1):
            out = jnp.where(row == r, blocks8[r], out)
        return out

    gate_dt = MXU_DTYPE
    pack = 4 // jnp.dtype(gate_dt).itemsize

    def store(ref, hd, ls, blocks):
        for i in range(0, nblk, pack):
            v = jnp.concatenate(blocks[i:i + pack], axis=0).astype(gate_dt)
            ref[hd, i // pack * SUBLANES:(i // pack + 1) * SUBLANES, ls] = v if pack == 1 else pltpu.bitcast(v, F32)

    def store_words(ref, hd, ls, blocks):
        for i in range(nblk):
            v = blocks[i]
            if pack == 2:
                bits = pltpu.bitcast(v.astype(gate_dt).astype(F32), jnp.uint32)
                v = pltpu.bitcast(bits | (bits >> 16), F32)
            ref[hd, i * SUBLANES:(i + 1) * SUBLANES, ls] = v

    for hd in range(PEER_HEADS):
        base = hd * 2 * PEER_HALF
        s1 = _dot(keys_ref[0, hd], _mx(qt[base:base + PEER_HALF]))
        s2 = _dot(keys_ref[1, hd], _mx(qt[base + PEER_HALF:base + 2 * PEER_HALF]))
        for lc in range(tile // LANES):
            ls = slice(lc * LANES, (lc + 1) * LANES)
            a1 = [s1[i * SUBLANES:(i + 1) * SUBLANES, ls] for i in range(nblk)]
            a2 = [s2[i * SUBLANES:(i + 1) * SUBLANES, ls] for i in range(nblk)]
            b1 = _top16(a1)
            b2 = _top16(a2)
            c1lo, c1hi = stack(b1[:SUBLANES]), stack(b1[SUBLANES:])
            c2lo, c2hi = stack(b2[:SUBLANES]), stack(b2[SUBLANES:])
            cand = []
            for a in range(4):
                cand += [b1[a] + c2lo, b1[a] + c2hi]
            c1lo_rest = jnp.where(row < 4, NEG, c1lo)
            for b in range(4):
                cand += [b2[b] + c1lo_rest, b2[b] + c1hi]
            top = _top16(cand)
            tau = top[PEER_TOPK - 1]
            zsum = jnp.ones_like(tau)
            for r in range(1, PEER_TOPK):
                zsum = zsum + jnp.exp(top[r] - top[0])
            inv_z = 1.0 / zsum
            theta = []
            for b in range(SUBLANES):
                v = jnp.minimum(jnp.where(c1lo + b2[b] >= tau, c1lo, -NEG), jnp.where(c1hi + b2[b] >= tau, c1hi, -NEG))
                theta.append(_all_sublanes(v, jnp.minimum))
            n_hi = _all_sublanes(jnp.where(b1[0] + c2hi >= tau, 1.0, 0.0), jnp.add)
            n_blocks, coef_blocks, r2_blocks, e2_blocks = [], [], [], []
            for i in range(nblk):
                cnt = jnp.zeros((SUBLANES, LANES), F32)
                for b in range(SUBLANES):
                    cnt = jnp.where(a1[i] >= theta[b], float(b + 1), cnt)
                n_blocks.append(cnt + jnp.where(a1[i] == b1[0], n_hi, 0.0))
                coef_blocks.append(jnp.exp(a1[i] - b1[0]) * inv_z)
                rank = jnp.full((SUBLANES, LANES), float(PEER_TOPK), F32)
                for b in range(PEER_TOPK - 1, -1, -1):
                    rank = jnp.where(a2[i] >= b2[b], float(b), rank)
                r2_blocks.append(rank)
                e2_blocks.append(jnp.exp(a2[i] - b2[0]))
            store_words(n_ref, hd, ls, n_blocks)
            store_words(coef_ref, hd, ls, coef_blocks)
            store(r2_ref, hd, ls, r2_blocks)
            store(e2_ref, hd, ls, e2_blocks)


def _peer_route(x2, g, w_q, sub_keys, tile):
    n = x2.shape[0]
    wqt = _mx(w_q.T)
    keys = _mx(sub_keys)
    pack = 4 // jnp.dtype(MXU_DTYPE).itemsize
    tok = pl.BlockSpec((PEER_HEADS, PEER_N_KEYS, tile), lambda i: (0, 0, i))
    tokp = pl.BlockSpec((PEER_HEADS, PEER_N_KEYS // pack, tile), lambda i: (0, 0, i))
    packed = jax.ShapeDtypeStruct((PEER_HEADS, PEER_N_KEYS // pack, n), F32)
    words = jax.ShapeDtypeStruct((PEER_HEADS, PEER_N_KEYS, n), F32)
    return pl.pallas_call(
        functools.partial(_peer_route_kernel, tile=tile),
        grid=(n // tile,),
        in_specs=[pl.BlockSpec((tile, D_MODEL), lambda i: (i, 0)), _full((1, D_MODEL)), _full(wqt.shape), _full(keys.shape)],
        out_specs=[pl.BlockSpec((D_MODEL, tile), lambda i: (0, i)), tokp, tokp, tok, tok],
        out_shape=[jax.ShapeDtypeStruct((D_MODEL, n), MXU_DTYPE), packed, packed, words, words],
        compiler_params=_params(1, 1),
        name="peer_route",
    )(x2, g.reshape(1, D_MODEL), wqt, keys)


def _peer_expert_kernel(ht_ref, u_ref, vt_ref, n_ref, coef_ref, r2_ref, e2_ref, x_ref, out_ref,
                        acc_ref, g_ref, *, tile, ib):
    step = pl.program_id(1)

    @pl.when(step == 0)
    def _():
        acc_ref[...] = jnp.zeros_like(acc_ref)

    gate_dt = g_ref.dtype
    pack = 4 // jnp.dtype(gate_dt).itemsize
    rows_per_vreg = SUBLANES * pack
    part = PEER_N_KEYS // EXPERT_I2_PARTS
    shape3 = (part // rows_per_vreg, rows_per_vreg, LANES)
    zero = jnp.zeros(shape3, gate_dt)

    ht = ht_ref[...]

    def row_vreg(ref, hd, k, ls):
        word = jnp.broadcast_to(ref[hd, k:k + 1, ls], (SUBLANES, LANES))
        return word if pack == 1 else pltpu.bitcast(word, gate_dt)

    def table(ref, hd, p, ls):
        words = ref[hd, p * part // pack:(p + 1) * part // pack, ls]
        return (words if pack == 1 else pltpu.bitcast(words, gate_dt)).reshape(shape3)

    n_groups = ib // EXPERT_I1_GROUP
    groups_per_chunk = n_groups // EXPERT_OUT_CHUNKS

    for kg in range(n_groups):
        ks = range(kg * EXPERT_I1_GROUP, (kg + 1) * EXPERT_I1_GROUP)
        act = {k: _gelu(_dot(u_ref[k * PEER_N_KEYS:(k + 1) * PEER_N_KEYS, :], ht).astype(gate_dt)) for k in ks}
        for lc in range(tile // LANES):
            ls = slice(lc * LANES, (lc + 1) * LANES)
            for p in range(EXPERT_I2_PARTS):
                w = {k: zero for k in ks}
                for hd in range(PEER_HEADS):
                    r2 = table(r2_ref, hd, p, ls)
                    e2 = table(e2_ref, hd, p, ls)
                    for k in ks:
                        hit = r2 < row_vreg(n_ref, hd, k, ls)[None]
                        w[k] = w[k] + jnp.where(hit, e2, zero) * row_vreg(coef_ref, hd, k, ls)[None]
                for k in ks:
                    a = act[k][p * part:(p + 1) * part, ls].astype(gate_dt).reshape(shape3)
                    rows = slice(k * PEER_N_KEYS + p * part, k * PEER_N_KEYS + (p + 1) * part)
                    g_ref[rows, ls] = (w[k] * a).reshape(part, LANES)
        if (kg + 1) % groups_per_chunk == 0:
            c = (kg + 1) // groups_per_chunk - 1
            crows = slice(c * groups_per_chunk * EXPERT_I1_GROUP * PEER_N_KEYS,
                          (c + 1) * groups_per_chunk * EXPERT_I1_GROUP * PEER_N_KEYS)
            acc_ref[...] += _dot(vt_ref[:, crows], g_ref[crows, :])

    @pl.when(step == pl.num_programs(1) - 1)
    def _():
        out_ref[...] = x_ref[...] + acc_ref[...].T


def _peer_experts(x2, ht, r2, e2, cnt, coef, u, vt, tile, ib):
    n = x2.shape[0]
    blk = ib * PEER_N_KEYS
    once = pl.Buffered(1)
    small = pl.BlockSpec((PEER_HEADS, ib, tile), lambda i, j: (0, j, i))
    big = pl.BlockSpec((PEER_HEADS, r2.shape[1], tile), lambda i, j: (0, 0, i), pipeline_mode=once)
    return pl.pallas_call(
        functools.partial(_peer_expert_kernel, tile=tile, ib=ib),
        grid=(n // tile, PEER_N_KEYS // ib),
        in_specs=[pl.BlockSpec((D_MODEL, tile), lambda i, j: (0, i), pipeline_mode=once),
                  pl.BlockSpec((blk, D_MODEL), lambda i, j: (j, 0)),
                  pl.BlockSpec((D_MODEL, blk), lambda i, j: (0, j)),
                  small, small, big, big,
                  pl.BlockSpec((tile, D_MODEL), lambda i, j: (i, 0), pipeline_mode=once)],
        out_specs=pl.BlockSpec((tile, D_MODEL), lambda i, j: (i, 0)),
        out_shape=jax.ShapeDtypeStruct((n, D_MODEL), F32),
        scratch_shapes=[pltpu.VMEM((D_MODEL, tile), F32), pltpu.VMEM((blk, tile), MXU_DTYPE)],
        compiler_params=_params(1, 2),
        name="peer_experts",
    )(ht, u, vt, cnt, coef, r2, e2, x2)


def _peer(x2, g, w_q, sub_keys, expert_u, expert_v, route_tile, tile, ib):
    ht, r2, e2, cnt, coef = _peer_route(x2, g, w_q, sub_keys, route_tile)
    return _peer_experts(x2, ht, r2, e2, cnt, coef, _mx(expert_u), _mx(expert_v.T), tile, ib)


def _tile(n, want):
    t = min(n, want)
    assert n % t == 0, (n, t)
    return t


def kernel(x, positions, mix_norm_even, w_in, gm_v_gain, gm_w_s, gm_b_s, mla_q_lat_gain, mla_w_uq, mla_kv_lat_gain, mla_w_ukv, mla_q_gain, mla_k_gain, w_out, mix_norm_odd, conv_w_pw1, conv_b_pw1, conv_w_dw, conv_b_dw, conv_norm_gain, conv_w_pw2, conv_b_pw2, ffn_norm, peer_w_q, peer_sub_keys, peer_u, peer_v):
    batch, seq, d = x.shape
    n = batch * seq
    depth = ffn_norm.shape[0]
    x2 = x.reshape(n, d)
    pos2 = positions.reshape(n, 1)
    row_tile = _tile(n, 512)
    for layer in range(depth):
        i = layer // 2
        if layer % 2 == 0:
            ug, vn, q, k, v = _mixer_in(x2, pos2, mix_norm_even[i], w_in[i], gm_v_gain[i], mla_q_lat_gain[i],
                                        mla_w_uq[i], mla_kv_lat_gain[i], mla_w_ukv[i], mla_q_gain[i],
                                        mla_k_gain[i], row_tile)
            o = _attention(q, k, v, batch, seq, _tile(seq, 256))
            x2 = _mixer_out(ug, vn, o, x2, gm_w_s[i], gm_b_s[i], w_out[i], row_tile)
        else:
            x2 = _conformer(x2, mix_norm_odd[i], conv_w_pw1[i], conv_b_pw1[i], conv_w_dw[i], conv_b_dw[i],
                            conv_norm_gain[i], conv_w_pw2[i], conv_b_pw2[i], batch, seq, _tile(seq, 256))
        x2 = _peer(x2, ffn_norm[layer], peer_w_q[layer], peer_sub_keys[layer], peer_u[layer], peer_v[layer],
                   _tile(n, 256), row_tile, 32)
    return x2.reshape(batch, seq, d)
```

```python
import functools
import math

import jax
import jax.numpy as jnp
from jax import lax
from jax.experimental import pallas as pl
from jax.experimental.pallas import tpu as pltpu

F32 = jnp.float32
MXU_DTYPE = jnp.bfloat16

D_MODEL = 1024
EPS = 1e-6
NEG = -1e30

GM_HEADS = 8
GM_HEAD_DIM = 64
GM_WIDTH = 512
GM_CHUNK = 128

MLA_HEADS = 8
MLA_Q_RANK = 384
MLA_KV_RANK = 256
MLA_NOPE_DIM = 64
MLA_ROPE_DIM = 32
MLA_QK_DIM = 96
MLA_V_DIM = 64
MLA_WIDTH = 512
ROPE_THETA = 10000.0
HEAD_PAD = 128

IN_WIDTH = 1696
IN_WIDTH_PAD = 1792

CONV_CH = 1024
CONV_WIDTH = 31
CONV_HALO = 32

PEER_HEADS = 8
PEER_N_KEYS = 128
PEER_HALF = 128
PEER_TOPK = 16

LANES = 128
VMEM_LIMIT = 56 * 1024 * 1024


def _dot(a, b):
    return jnp.dot(a, b, preferred_element_type=F32)


def _dot_nt(a, b):
    return lax.dot_general(a, b, (((1,), (1,)), ((), ())), preferred_element_type=F32)


def _mx(a):
    return a.astype(MXU_DTYPE)


def _gelu(x):
    c = math.sqrt(2.0 / math.pi)
    return x * (0.5 + 0.5 * jnp.tanh(x * (c + (c * 0.044715) * (x * x))))


def _rms(x, g):
    ms = jnp.mean(x * x, axis=-1, keepdims=True)
    return x * lax.rsqrt(ms + EPS) * g


def _seg_sum(y, seg):
    hi = _mx(y)
    lo = _mx(y - hi.astype(F32))
    return _dot(hi, seg) + _dot(lo, seg)


def _params(n_parallel_axes, n_axes):
    sem = ("parallel",) * n_parallel_axes + ("arbitrary",) * (n_axes - n_parallel_axes)
    return pltpu.CompilerParams(dimension_semantics=sem, vmem_limit_bytes=VMEM_LIMIT)


def _full(shape):
    nd = len(shape)
    return pl.BlockSpec(shape, lambda *_: (0,) * nd)


def _mixer_in_kernel(x_ref, pos_ref, g_ref, win_ref, vseg_ref, vgain_ref, qlg_ref, wuq_ref, kvlg_ref,
                     wuk_ref, wuv_ref, eplace_ref, rot_ref, ones_ref, qg_ref, kg_ref, invf_ref,
                     ug_ref, vn_ref, q_ref, k_ref, v_ref):
    h = _mx(_rms(x_ref[...], g_ref[...]))
    z = _dot(h, win_ref[...])
    u = z[:, :GM_WIDTH]
    v = z[:, GM_WIDTH:2 * GM_WIDTH]
    q_lat = z[:, 2 * GM_WIDTH:2 * GM_WIDTH + MLA_Q_RANK]
    kv_lat = z[:, 2 * GM_WIDTH + MLA_Q_RANK:2 * GM_WIDTH + MLA_Q_RANK + MLA_KV_RANK]
    k_pe = z[:, IN_WIDTH_PAD - LANES:]

    ug_ref[...] = _gelu(u).astype(ug_ref.dtype)
    gv = _gelu(v)
    vseg = vseg_ref[...]
    for s in range(GM_WIDTH // LANES):
        sl = slice(s * LANES, (s + 1) * LANES)
        y = gv[:, sl]
        ms = _seg_sum(y * y, vseg) * (1.0 / GM_HEAD_DIM)
        vn_ref[:, sl] = (y * lax.rsqrt(ms + EPS) * vgain_ref[:, sl]).astype(vn_ref.dtype)

    ang = pos_ref[...].astype(F32) * invf_ref[...]
    cos = jnp.cos(ang)
    sin = jnp.sin(ang)
    rot = rot_ref[...]
    ones = ones_ref[...]

    def qk_norm_rope(raw, gain, scale):
        ms = _seg_sum(raw * raw, ones) * (1.0 / MLA_QK_DIM)
        y = raw * lax.rsqrt(ms + EPS) * gain
        return (y * cos + _dot(_mx(y), rot) * sin) * scale

    qn = _mx(_rms(q_lat, qlg_ref[...]))
    q_raw = _dot(qn, wuq_ref[...])
    kvn = _mx(_rms(kv_lat, kvlg_ref[...]))
    k_raw = _dot(kvn, wuk_ref[...]) + _dot(_mx(k_pe), eplace_ref[...])
    v_ref[...] = _dot(kvn, wuv_ref[...]).astype(v_ref.dtype)
    q_scale = MLA_QK_DIM ** -0.5
    for hd in range(MLA_HEADS):
        sl = slice(hd * HEAD_PAD, (hd + 1) * HEAD_PAD)
        q_ref[:, sl] = qk_norm_rope(q_raw[:, sl], qg_ref[...], q_scale).astype(q_ref.dtype)
        k_ref[:, sl] = qk_norm_rope(k_raw[:, sl], kg_ref[...], 1.0).astype(k_ref.dtype)


def _mixer_in(x2, pos2, g, w_in, gm_v_gain, q_lat_gain, w_uq, kv_lat_gain, w_ukv, q_gain, k_gain, tile):
    n = x2.shape[0]
    win = _mx(jnp.pad(w_in, ((0, 0), (0, IN_WIDTH_PAD - IN_WIDTH))))
    lane = jnp.arange(LANES)
    vseg = _mx((lane[:, None] // GM_HEAD_DIM == lane[None, :] // GM_HEAD_DIM).astype(F32))
    ones = _mx(jnp.ones((LANES, LANES), F32))
    vgain = gm_v_gain.reshape(1, GM_WIDTH)
    wuq = w_uq.reshape(MLA_Q_RANK, MLA_HEADS, MLA_QK_DIM)
    wuq = _mx(jnp.pad(wuq, ((0, 0), (0, 0), (0, HEAD_PAD - MLA_QK_DIM))).reshape(MLA_Q_RANK, MLA_HEADS * HEAD_PAD))
    wukv = w_ukv.reshape(MLA_KV_RANK, MLA_HEADS, MLA_NOPE_DIM + MLA_V_DIM)
    wuk = jnp.pad(wukv[:, :, :MLA_NOPE_DIM], ((0, 0), (0, 0), (0, HEAD_PAD - MLA_NOPE_DIM)))
    wuk = _mx(wuk.reshape(MLA_KV_RANK, MLA_HEADS * HEAD_PAD))
    wuv = _mx(wukv[:, :, MLA_NOPE_DIM:].reshape(MLA_KV_RANK, MLA_WIDTH))
    r = jnp.arange(MLA_ROPE_DIM)
    eplace = jnp.zeros((LANES, MLA_HEADS, HEAD_PAD), F32).at[r, :, MLA_NOPE_DIM + r].set(1.0)
    eplace = _mx(eplace.reshape(LANES, MLA_HEADS * HEAD_PAD))
    half = MLA_ROPE_DIM // 2
    j = jnp.arange(half)
    rot = jnp.zeros((HEAD_PAD, HEAD_PAD), F32)
    rot = rot.at[MLA_NOPE_DIM + half + j, MLA_NOPE_DIM + j].set(-1.0).at[MLA_NOPE_DIM + j, MLA_NOPE_DIM + half + j].set(1.0)
    rot = _mx(rot)
    inv_freq = ROPE_THETA ** (-jnp.arange(half, dtype=F32) / half)
    invf = jnp.zeros((1, HEAD_PAD), F32).at[0, MLA_NOPE_DIM + j].set(inv_freq).at[0, MLA_NOPE_DIM + half + j].set(inv_freq)
    qg = jnp.pad(q_gain, (0, HEAD_PAD - MLA_QK_DIM)).reshape(1, HEAD_PAD)
    kg = jnp.pad(k_gain, (0, HEAD_PAD - MLA_QK_DIM)).reshape(1, HEAD_PAD)

    row = lambda w: pl.BlockSpec((tile, w), lambda i: (i, 0))
    consts = (g.reshape(1, D_MODEL), win, vseg, vgain, q_lat_gain.reshape(1, -1), wuq, kv_lat_gain.reshape(1, -1),
              wuk, wuv, eplace, rot, ones, qg, kg, invf)
    out_dt = MXU_DTYPE
    return pl.pallas_call(
        _mixer_in_kernel,
        grid=(n // tile,),
        in_specs=[row(D_MODEL), row(1)] + [_full(c.shape) for c in consts],
        out_specs=[row(GM_WIDTH), row(GM_WIDTH), row(MLA_HEADS * HEAD_PAD), row(MLA_HEADS * HEAD_PAD), row(MLA_WIDTH)],
        out_shape=[jax.ShapeDtypeStruct((n, GM_WIDTH), out_dt), jax.ShapeDtypeStruct((n, GM_WIDTH), out_dt),
                   jax.ShapeDtypeStruct((n, MLA_HEADS * HEAD_PAD), out_dt),
                   jax.ShapeDtypeStruct((n, MLA_HEADS * HEAD_PAD), out_dt),
                   jax.ShapeDtypeStruct((n, MLA_WIDTH), out_dt)],
        compiler_params=_params(1, 1),
        name="mixer_in",
    )(x2, pos2, *consts)


def _attention_kernel(q_ref, k_ref, v_ref, o_ref, *, tq):
    qi = pl.program_id(2)
    lane = lax.broadcasted_iota(jnp.int32, (tq, LANES), 1)
    heads = (slice(0, HEAD_PAD), slice(HEAD_PAD, 2 * HEAD_PAD))
    qs = [q_ref[:, hs] for hs in heads]

    def scores(j, hh):
        kj = k_ref[pl.ds(pl.multiple_of(j * tq, tq), tq), heads[hh]]
        return _dot_nt(qs[hh], kj)

    def update(carry, s, vj):
        m, l, acc = carry
        m_new = jnp.maximum(m, jnp.max(s, axis=-1, keepdims=True))
        alpha = jnp.exp(m - m_new)
        p = jnp.exp(s - m_new)
        acc = acc * alpha + _dot(_mx(p), vj)
        return m_new, l * alpha + jnp.sum(p, axis=-1, keepdims=True), acc

    def body(j, carry):
        vj = v_ref[pl.ds(pl.multiple_of(j * tq, tq), tq), :]
        return tuple(update(carry[hh], scores(j, hh), vj) for hh in range(2))

    init = (jnp.full((tq, 1), NEG, F32), jnp.zeros((tq, 1), F32), jnp.zeros((tq, LANES), F32))
    carry = lax.fori_loop(0, qi, body, (init, init))
    r = lax.broadcasted_iota(jnp.int32, (tq, tq), 0)
    c = lax.broadcasted_iota(jnp.int32, (tq, tq), 1)
    vd = v_ref[pl.ds(pl.multiple_of(qi * tq, tq), tq), :]
    outs = []
    for hh in range(2):
        m, l, acc = update(carry[hh], jnp.where(r >= c, scores(qi, hh), NEG), vd)
        outs.append(acc / l)
    o_ref[...] = jnp.where(lane < MLA_V_DIM, outs[0], outs[1]).astype(o_ref.dtype)


def _attention(q, k, v, batch, seq, tq):
    n = batch * seq
    nq = seq // tq
    return pl.pallas_call(
        functools.partial(_attention_kernel, tq=tq),
        grid=(batch, MLA_HEADS // 2, nq),
        in_specs=[pl.BlockSpec((tq, 2 * HEAD_PAD), lambda b, hp, i: (b * nq + i, hp)),
                  pl.BlockSpec((seq, 2 * HEAD_PAD), lambda b, hp, i: (b, hp)),
                  pl.BlockSpec((seq, 2 * MLA_V_DIM), lambda b, hp, i: (b, hp))],
        out_specs=pl.BlockSpec((tq, 2 * MLA_V_DIM), lambda b, hp, i: (b * nq + i, hp)),
        out_shape=jax.ShapeDtypeStruct((n, MLA_WIDTH), MXU_DTYPE),
        compiler_params=_params(3, 3),
        name="attention",
    )(q, k, v)


def _mixer_out_kernel(ug_ref, vn_ref, o_ref, x_ref, ws_ref, bias_ref, woa_ref, wob_ref, out_ref, a_ref, *, tile):
    lane = lax.broadcasted_iota(jnp.int32, (GM_CHUNK, LANES), 1)
    left = lane < GM_HEAD_DIM
    for c in range(tile // GM_CHUNK):
        rows = slice(c * GM_CHUNK, (c + 1) * GM_CHUNK)
        for p in range(GM_WIDTH // LANES):
            sl = slice(p * LANES, (p + 1) * LANES)
            vs = vn_ref[rows, sl]
            zero = jnp.zeros_like(vs)
            rhs = jnp.concatenate([jnp.where(left, vs, zero), jnp.where(left, zero, vs)], axis=0)
            mixed = _dot(ws_ref[p], rhs) + bias_ref[:, sl]
            a_ref[rows, sl] = (ug_ref[rows, sl].astype(F32) * mixed).astype(a_ref.dtype)
    out_ref[...] = x_ref[...] + _dot(a_ref[...], woa_ref[...]) + _dot(o_ref[...], wob_ref[...])


def _mixer_out(ug, vn, o, x2, gm_w_s, gm_b_s, w_out, tile):
    n = x2.shape[0]
    causal = jnp.tril(jnp.ones((GM_CHUNK, GM_CHUNK), bool))
    w = jnp.where(causal[None], gm_w_s, 0.0)
    ws = _mx(w.reshape(GM_HEADS // 2, 2, GM_CHUNK, GM_CHUNK).transpose(0, 2, 1, 3).reshape(GM_HEADS // 2, GM_CHUNK, 2 * GM_CHUNK))
    bias = jnp.repeat(gm_b_s.T, GM_HEAD_DIM, axis=1)
    woa = _mx(w_out[:GM_WIDTH])
    wob = _mx(w_out[GM_WIDTH:])
    row = lambda w_: pl.BlockSpec((tile, w_), lambda i: (i, 0))
    return pl.pallas_call(
        functools.partial(_mixer_out_kernel, tile=tile),
        grid=(n // tile,),
        in_specs=[row(GM_WIDTH), row(GM_WIDTH), row(MLA_WIDTH), row(D_MODEL),
                  _full(ws.shape), _full(bias.shape), _full(woa.shape), _full(wob.shape)],
        out_specs=row(D_MODEL),
        out_shape=jax.ShapeDtypeStruct((n, D_MODEL), F32),
        scratch_shapes=[pltpu.VMEM((tile, GM_WIDTH), MXU_DTYPE)],
        compiler_params=_params(1, 1),
        name="mixer_out",
    )(ug, vn, o, x2, ws, bias, woa, wob)


def _conformer_kernel(x_ref, g_ref, w1_ref, b1_ref, wdw_ref, bdw_ref, ng_ref, w2_ref, b2_ref, out_ref, zext_ref, shift_ref, *, tile):
    @pl.when(pl.program_id(1) == 0)
    def _():
        zext_ref[0:CONV_HALO, :] = jnp.zeros((CONV_HALO, CONV_CH), F32)
        zext_ref[CONV_HALO + tile:, :] = jnp.zeros((SUBLANES, CONV_CH), F32)

    x = x_ref[...]
    h = _mx(_rms(x, g_ref[...]))
    z = _dot(h, w1_ref[...]) + b1_ref[...]
    glu = z[:, :CONV_CH] * jax.nn.sigmoid(z[:, CONV_CH:])
    zext_ref[CONV_HALO:CONV_HALO + tile, :] = glu
    first = CONV_HALO - (CONV_WIDTH - 1)
    cols = []
    for cb in range(CONV_CH // LANES):
        sl = slice(cb * LANES, (cb + 1) * LANES)
        acc = jnp.zeros((tile, LANES), F32)
        for r in range(SUBLANES):
            taps = [k for k in range(CONV_WIDTH) if (first + k) % SUBLANES == r]
            shift_ref[r] = zext_ref[r:r + CONV_HALO + tile, sl]
            for k in taps:
                off = first + k - r
                acc = acc + shift_ref[r, off:off + tile, :] * wdw_ref[k:k + 1, sl]
        cols.append(acc)
    y = jnp.concatenate(cols, axis=1) + bdw_ref[...]
    zext_ref[0:CONV_HALO, :] = zext_ref[tile:tile + CONV_HALO, :]
    yn = _rms(y, ng_ref[...])
    s = yn * jax.nn.sigmoid(yn)
    out_ref[...] = x + _dot(_mx(s), w2_ref[...]) + b2_ref[...]


def _conformer(x2, g, w_pw1, b_pw1, w_dw, b_dw, norm_gain, w_pw2, b_pw2, batch, seq, tile):
    n = x2.shape[0]
    nt = seq // tile
    consts = (g.reshape(1, -1), _mx(w_pw1), b_pw1.reshape(1, -1), w_dw, b_dw.reshape(1, -1),
              norm_gain.reshape(1, -1), _mx(w_pw2), b_pw2.reshape(1, -1))
    row = pl.BlockSpec((tile, D_MODEL), lambda b, i: (b * nt + i, 0))
    return pl.pallas_call(
        functools.partial(_conformer_kernel, tile=tile),
        grid=(batch, nt),
        in_specs=[row] + [_full(c.shape) for c in consts],
        out_specs=row,
        out_shape=jax.ShapeDtypeStruct((n, D_MODEL), F32),
        scratch_shapes=[pltpu.VMEM((CONV_HALO + tile + SUBLANES, CONV_CH), F32),
                        pltpu.VMEM((SUBLANES, CONV_HALO + tile, LANES), F32)],
        compiler_params=_params(0, 2),
        name="conformer",
    )(x2, *consts)


def _oddeven_merge(lo, hi, r):
    step = r * 2
    if step < hi - lo:
        yield from _oddeven_merge(lo, hi, step)
        yield from _oddeven_merge(lo + r, hi, step)
        yield from [(i, i + r) for i in range(lo + r, hi - r, step)]
    else:
        yield (lo, lo + r)


def _oddeven_sort(lo, hi):
    if hi - lo >= 1:
        mid = lo + (hi - lo) // 2
        yield from _oddeven_sort(lo, mid)
        yield from _oddeven_sort(mid + 1, hi)
        yield from _oddeven_merge(lo, hi, 1)


SORT16 = tuple(_oddeven_sort(0, PEER_TOPK - 1))
BITONIC16 = tuple((i, i + d) for d in (8, 4, 2, 1) for i in range(PEER_TOPK) if not i & d)
SUBLANES = 8
EXPERT_I1_GROUP = 2
EXPERT_I2_PARTS = 1
EXPERT_DOT_I1 = 8
EXPERT_OUT_CHUNKS = 1


def _compare_exchange(b, i, j):
    b[i], b[j] = jnp.maximum(b[i], b[j]), jnp.minimum(b[i], b[j])


def _top16(blocks):
    b = list(blocks)
    for i, j in SORT16:
        _compare_exchange(b, i, j)
    for shift in (4, 2, 1):
        b = [jnp.maximum(b[i], pltpu.roll(b[PEER_TOPK - 1 - i], shift, axis=0)) for i in range(PEER_TOPK)]
        for i, j in BITONIC16:
            _compare_exchange(b, i, j)
    return b


def _all_sublanes(x, op):
    for shift in (4, 2, 1):
        x = op(x, pltpu.roll(x, shift, axis=0))
    return x


def _peer_route_kernel(x_ref, g_ref, wqt_ref, keys_ref, ht_ref, r2_ref, e2_ref, n_ref, coef_ref, *, tile):
    h = _rms(x_ref[...], g_ref[...])
    ht = _mx(h.T)
    ht_ref[...] = ht
    qt = _dot(wqt_ref[...], ht)
    row = lax.broadcasted_iota(jnp.int32, (SUBLANES, LANES), 0)
    nblk = PEER_N_KEYS // SUBLANES

    def stack(blocks8):
        out = blocks8[SUBLANES - 1]
        for r in range(SUBLANES - 2, -1, -1):
            out = jnp.where(row == r, blocks8[r], out)
        return out

    gate_dt = MXU_DTYPE
    pack = 4 // jnp.dtype(gate_dt).itemsize

    def store(ref, hd, ls, blocks):
        for i in range(0, nblk, pack):
            v = jnp.concatenate(blocks[i:i + pack], axis=0).astype(gate_dt)
            ref[hd, i // pack * SUBLANES:(i // pack + 1) * SUBLANES, ls] = v if pack == 1 else pltpu.bitcast(v, F32)

    def store_words(ref, hd, ls, blocks):
        for i in range(nblk):
            v = blocks[i]
            if pack == 2:
                bits = pltpu.bitcast(v.astype(gate_dt).astype(F32), jnp.uint32)
                v = pltpu.bitcast(bits | (bits >> 16), F32)
            ref[hd, i * SUBLANES:(i + 1) * SUBLANES, ls] = v

    for hd in range(PEER_HEADS):
        base = hd * 2 * PEER_HALF
        s1 = _dot(keys_ref[0, hd], _mx(qt[base:base + PEER_HALF]))
        s2 = _dot(keys_ref[1, hd], _mx(qt[base + PEER_HALF:base + 2 * PEER_HALF]))
        for lc in range(tile // LANES):
            ls = slice(lc * LANES, (lc + 1) * LANES)
            a1 = [s1[i * SUBLANES:(i + 1) * SUBLANES, ls] for i in range(nblk)]
            a2 = [s2[i * SUBLANES:(i + 1) * SUBLANES, ls] for i in range(nblk)]
            b1 = _top16(a1)
            b2 = _top16(a2)
            c1lo, c1hi = stack(b1[:SUBLANES]), stack(b1[SUBLANES:])
            c2lo, c2hi = stack(b2[:SUBLANES]), stack(b2[SUBLANES:])
            cand = []
            for a in range(4):
                cand += [b1[a] + c2lo, b1[a] + c2hi]
            c1lo_rest = jnp.where(row < 4, NEG, c1lo)
            for b in range(4):
                cand += [b2[b] + c1lo_rest, b2[b] + c1hi]
            top = _top16(cand)
            tau = top[PEER_TOPK - 1]
            zsum = jnp.ones_like(tau)
            for r in range(1, PEER_TOPK):
                zsum = zsum + jnp.exp(top[r] - top[0])
            inv_z = 1.0 / zsum
            theta = []
            for b in range(SUBLANES):
                v = jnp.minimum(jnp.where(c1lo + b2[b] >= tau, c1lo, -NEG), jnp.where(c1hi + b2[b] >= tau, c1hi, -NEG))
                theta.append(_all_sublanes(v, jnp.minimum))
            n_hi = _all_sublanes(jnp.where(b1[0] + c2hi >= tau, 1.0, 0.0), jnp.add)
            n_blocks, coef_blocks, r2_blocks, e2_blocks = [], [], [], []
            for i in range(nblk):
                cnt = jnp.zeros((SUBLANES, LANES), F32)
                for b in range(SUBLANES):
                    cnt = jnp.where(a1[i] >= theta[b], float(b + 1), cnt)
                n_blocks.append(cnt + jnp.where(a1[i] == b1[0], n_hi, 0.0))
                coef_blocks.append(jnp.exp(a1[i] - b1[0]) * inv_z)
                rank = jnp.full((SUBLANES, LANES), float(PEER_TOPK), F32)
                for b in range(PEER_TOPK - 1, -1, -1):
                    rank = jnp.where(a2[i] >= b2[b], float(b), rank)
                r2_blocks.append(rank)
                e2_blocks.append(jnp.exp(a2[i] - b2[0]))
            store_words(n_ref, hd, ls, n_blocks)
            store_words(coef_ref, hd, ls, coef_blocks)
            store(r2_ref, hd, ls, r2_blocks)
            store(e2_ref, hd, ls, e2_blocks)


def _peer_route(x2, g, w_q, sub_keys, tile):
    n = x2.shape[0]
    wqt = _mx(w_q.T)
    keys = _mx(sub_keys)
    pack = 4 // jnp.dtype(MXU_DTYPE).itemsize
    tok = pl.BlockSpec((PEER_HEADS, PEER_N_KEYS, tile), lambda i: (0, 0, i))
    tokp = pl.BlockSpec((PEER_HEADS, PEER_N_KEYS // pack, tile), lambda i: (0, 0, i))
    packed = jax.ShapeDtypeStruct((PEER_HEADS, PEER_N_KEYS // pack, n), F32)
    words = jax.ShapeDtypeStruct((PEER_HEADS, PEER_N_KEYS, n), F32)
    return pl.pallas_call(
        functools.partial(_peer_route_kernel, tile=tile),
        grid=(n // tile,),
        in_specs=[pl.BlockSpec((tile, D_MODEL), lambda i: (i, 0)), _full((1, D_MODEL)), _full(wqt.shape), _full(keys.shape)],
        out_specs=[pl.BlockSpec((D_MODEL, tile), lambda i: (0, i)), tokp, tokp, tok, tok],
        out_shape=[jax.ShapeDtypeStruct((D_MODEL, n), MXU_DTYPE), packed, packed, words, words],
        compiler_params=_params(1, 1),
        name="peer_route",
    )(x2, g.reshape(1, D_MODEL), wqt, keys)


def _peer_expert_kernel(ht_ref, u_ref, vt_ref, n_ref, coef_ref, r2_ref, e2_ref, x_ref, out_ref,
                        acc_ref, g_ref, a_ref, *, tile, ib):
    step = pl.program_id(1)

    @pl.when(step == 0)
    def _():
        acc_ref[...] = jnp.zeros_like(acc_ref)

    gate_dt = g_ref.dtype
    pack = 4 // jnp.dtype(gate_dt).itemsize
    rows_per_vreg = SUBLANES * pack
    part = PEER_N_KEYS // EXPERT_I2_PARTS
    shape3 = (part // rows_per_vreg, rows_per_vreg, LANES)
    zero = jnp.zeros(shape3, gate_dt)

    ht = ht_ref[...]

    def row_vreg(ref, hd, k, ls):
        word = jnp.broadcast_to(ref[hd, k:k + 1, ls], (SUBLANES, LANES))
        return word if pack == 1 else pltpu.bitcast(word, gate_dt)

    def table(ref, hd, p, ls):
        words = ref[hd, p * part // pack:(p + 1) * part // pack, ls]
        return (words if pack == 1 else pltpu.bitcast(words, gate_dt)).reshape(shape3)

    n_groups = ib // EXPERT_I1_GROUP
    groups_per_chunk = n_groups // EXPERT_OUT_CHUNKS

    for kg in range(n_groups):
        ks = range(kg * EXPERT_I1_GROUP, (kg + 1) * EXPERT_I1_GROUP)
        if ks[0] % EXPERT_DOT_I1 == 0:
            drows = slice(ks[0] * PEER_N_KEYS, (ks[0] + EXPERT_DOT_I1) * PEER_N_KEYS)
            a_ref[drows, :] = _gelu(_dot(u_ref[drows, :], ht).astype(gate_dt))
        for lc in range(tile // LANES):
            ls = slice(lc * LANES, (lc + 1) * LANES)
            for p in range(EXPERT_I2_PARTS):
                w = {k: zero for k in ks}
                for hd in range(PEER_HEADS):
                    r2 = table(r2_ref, hd, p, ls)
                    e2 = table(e2_ref, hd, p, ls)
                    for k in ks:
                        hit = r2 < row_vreg(n_ref, hd, k, ls)[None]
                        w[k] = w[k] + jnp.where(hit, e2, zero) * row_vreg(coef_ref, hd, k, ls)[None]
                for k in ks:
                    rows = slice(k * PEER_N_KEYS + p * part, k * PEER_N_KEYS + (p + 1) * part)
                    g_ref[rows, ls] = (w[k] * a_ref[rows, ls].reshape(shape3)).reshape(part, LANES)
        if (kg + 1) % groups_per_chunk == 0:
            c = (kg + 1) // groups_per_chunk - 1
            crows = slice(c * groups_per_chunk * EXPERT_I1_GROUP * PEER_N_KEYS,
                          (c + 1) * groups_per_chunk * EXPERT_I1_GROUP * PEER_N_KEYS)
            acc_ref[...] += _dot(vt_ref[:, crows], g_ref[crows, :])

    @pl.when(step == pl.num_programs(1) - 1)
    def _():
        out_ref[...] = x_ref[...] + acc_ref[...].T


def _peer_experts(x2, ht, r2, e2, cnt, coef, u, vt, tile, ib):
    n = x2.shape[0]
    blk = ib * PEER_N_KEYS
    once = pl.Buffered(1)
    small = pl.BlockSpec((PEER_HEADS, ib, tile), lambda i, j: (0, j, i))
    big = pl.BlockSpec((PEER_HEADS, r2.shape[1], tile), lambda i, j: (0, 0, i), pipeline_mode=once)
    return pl.pallas_call(
        functools.partial(_peer_expert_kernel, tile=tile, ib=ib),
        grid=(n // tile, PEER_N_KEYS // ib),
        in_specs=[pl.BlockSpec((D_MODEL, tile), lambda i, j: (0, i), pipeline_mode=once),
                  pl.BlockSpec((blk, D_MODEL), lambda i, j: (j, 0)),
                  pl.BlockSpec((D_MODEL, blk), lambda i, j: (0, j)),
                  small, small, big, big,
                  pl.BlockSpec((tile, D_MODEL), lambda i, j: (i, 0), pipeline_mode=once)],
        out_specs=pl.BlockSpec((tile, D_MODEL), lambda i, j: (i, 0)),
        out_shape=jax.ShapeDtypeStruct((n, D_MODEL), F32),
        scratch_shapes=[pltpu.VMEM((D_MODEL, tile), F32), pltpu.VMEM((blk, tile), MXU_DTYPE),
                        pltpu.VMEM((blk, tile), MXU_DTYPE)],
        compiler_params=_params(1, 2),
        name="peer_experts",
    )(ht, u, vt, cnt, coef, r2, e2, x2)


def _peer(x2, g, w_q, sub_keys, expert_u, expert_v, route_tile, tile, ib):
    ht, r2, e2, cnt, coef = _peer_route(x2, g, w_q, sub_keys, route_tile)
    return _peer_experts(x2, ht, r2, e2, cnt, coef, _mx(expert_u), _mx(expert_v.T), tile, ib)


def _tile(n, want):
    t = min(n, want)
    assert n % t == 0, (n, t)
    return t


def kernel(x, positions, mix_norm_even, w_in, gm_v_gain, gm_w_s, gm_b_s, mla_q_lat_gain, mla_w_uq, mla_kv_lat_gain, mla_w_ukv, mla_q_gain, mla_k_gain, w_out, mix_norm_odd, conv_w_pw1, conv_b_pw1, conv_w_dw, conv_b_dw, conv_norm_gain, conv_w_pw2, conv_b_pw2, ffn_norm, peer_w_q, peer_sub_keys, peer_u, peer_v):
    batch, seq, d = x.shape
    n = batch * seq
    depth = ffn_norm.shape[0]
    x2 = x.reshape(n, d)
    pos2 = positions.reshape(n, 1)
    row_tile = _tile(n, 512)
    for layer in range(depth):
        i = layer // 2
        if layer % 2 == 0:
            ug, vn, q, k, v = _mixer_in(x2, pos2, mix_norm_even[i], w_in[i], gm_v_gain[i], mla_q_lat_gain[i],
                                        mla_w_uq[i], mla_kv_lat_gain[i], mla_w_ukv[i], mla_q_gain[i],
                                        mla_k_gain[i], row_tile)
            o = _attention(q, k, v, batch, seq, _tile(seq, 256))
            x2 = _mixer_out(ug, vn, o, x2, gm_w_s[i], gm_b_s[i], w_out[i], row_tile)
        else:
            x2 = _conformer(x2, mix_norm_odd[i], conv_w_pw1[i], conv_b_pw1[i], conv_w_dw[i], conv_b_dw[i],
                            conv_norm_gain[i], conv_w_pw2[i], conv_b_pw2[i], batch, seq, _tile(seq, 256))
        x2 = _peer(x2, ffn_norm[layer], peer_w_q[layer], peer_sub_keys[layer], peer_u[layer], peer_v[layer],
                   _tile(n, 256), row_tile, 32)
    return x2.reshape(batch, seq, d)
```

```python
import functools
import math

import jax
import jax.numpy as jnp
from jax import lax
from jax.experimental import pallas as pl
from jax.experimental.pallas import tpu as pltpu

F32 = jnp.float32
MXU_DTYPE = jnp.bfloat16

D_MODEL = 1024
EPS = 1e-6
NEG = -1e30

GM_HEADS = 8
GM_HEAD_DIM = 64
GM_WIDTH = 512
GM_CHUNK = 128

MLA_HEADS = 8
MLA_Q_RANK = 384
MLA_KV_RANK = 256
MLA_NOPE_DIM = 64
MLA_ROPE_DIM = 32
MLA_QK_DIM = 96
MLA_V_DIM = 64
MLA_WIDTH = 512
ROPE_THETA = 10000.0
HEAD_PAD = 128

IN_WIDTH = 1696
IN_WIDTH_PAD = 1792

CONV_CH = 1024
CONV_WIDTH = 31
CONV_HALO = 32

PEER_HEADS = 8
PEER_N_KEYS = 128
PEER_HALF = 128
PEER_TOPK = 16

LANES = 128
VMEM_LIMIT = 56 * 1024 * 1024


def _dot(a, b):
    return jnp.dot(a, b, preferred_element_type=F32)


def _dot_nt(a, b):
    return lax.dot_general(a, b, (((1,), (1,)), ((), ())), preferred_element_type=F32)


def _mx(a):
    return a.astype(MXU_DTYPE)


def _gelu(x):
    c = math.sqrt(2.0 / math.pi)
    return x * (0.5 + 0.5 * jnp.tanh(x * (c + (c * 0.044715) * (x * x))))


def _rms(x, g):
    ms = jnp.mean(x * x, axis=-1, keepdims=True)
    return x * lax.rsqrt(ms + EPS) * g


def _seg_sum(y, seg):
    hi = _mx(y)
    lo = _mx(y - hi.astype(F32))
    return _dot(hi, seg) + _dot(lo, seg)


def _params(n_parallel_axes, n_axes):
    sem = ("parallel",) * n_parallel_axes + ("arbitrary",) * (n_axes - n_parallel_axes)
    return pltpu.CompilerParams(dimension_semantics=sem, vmem_limit_bytes=VMEM_LIMIT)


def _full(shape):
    nd = len(shape)
    return pl.BlockSpec(shape, lambda *_: (0,) * nd)


def _mixer_in_kernel(x_ref, pos_ref, g_ref, win_ref, vseg_ref, vgain_ref, qlg_ref, wuq_ref, kvlg_ref,
                     wuk_ref, wuv_ref, eplace_ref, rot_ref, ones_ref, qg_ref, kg_ref, invf_ref,
                     ug_ref, vn_ref, q_ref, k_ref, v_ref):
    h = _mx(_rms(x_ref[...], g_ref[...]))
    z = _dot(h, win_ref[...])
    u = z[:, :GM_WIDTH]
    v = z[:, GM_WIDTH:2 * GM_WIDTH]
    q_lat = z[:, 2 * GM_WIDTH:2 * GM_WIDTH + MLA_Q_RANK]
    kv_lat = z[:, 2 * GM_WIDTH + MLA_Q_RANK:2 * GM_WIDTH + MLA_Q_RANK + MLA_KV_RANK]
    k_pe = z[:, IN_WIDTH_PAD - LANES:]

    ug_ref[...] = _gelu(u).astype(ug_ref.dtype)
    gv = _gelu(v)
    vseg = vseg_ref[...]
    for s in range(GM_WIDTH // LANES):
        sl = slice(s * LANES, (s + 1) * LANES)
        y = gv[:, sl]
        ms = _seg_sum(y * y, vseg) * (1.0 / GM_HEAD_DIM)
        vn_ref[:, sl] = (y * lax.rsqrt(ms + EPS) * vgain_ref[:, sl]).astype(vn_ref.dtype)

    ang = pos_ref[...].astype(F32) * invf_ref[...]
    cos = jnp.cos(ang)
    sin = jnp.sin(ang)
    rot = rot_ref[...]
    ones = ones_ref[...]

    def qk_norm_rope(raw, gain, scale):
        ms = _seg_sum(raw * raw, ones) * (1.0 / MLA_QK_DIM)
        y = raw * lax.rsqrt(ms + EPS) * gain
        return (y * cos + _dot(_mx(y), rot) * sin) * scale

    qn = _mx(_rms(q_lat, qlg_ref[...]))
    q_raw = _dot(qn, wuq_ref[...])
    kvn = _mx(_rms(kv_lat, kvlg_ref[...]))
    k_raw = _dot(kvn, wuk_ref[...]) + _dot(_mx(k_pe), eplace_ref[...])
    v_ref[...] = _dot(kvn, wuv_ref[...]).astype(v_ref.dtype)
    q_scale = MLA_QK_DIM ** -0.5
    for hd in range(MLA_HEADS):
        sl = slice(hd * HEAD_PAD, (hd + 1) * HEAD_PAD)
        q_ref[:, sl] = qk_norm_rope(q_raw[:, sl], qg_ref[...], q_scale).astype(q_ref.dtype)
        k_ref[:, sl] = qk_norm_rope(k_raw[:, sl], kg_ref[...], 1.0).astype(k_ref.dtype)


def _mixer_in(x2, pos2, g, w_in, gm_v_gain, q_lat_gain, w_uq, kv_lat_gain, w_ukv, q_gain, k_gain, tile):
    n = x2.shape[0]
    win = _mx(jnp.pad(w_in, ((0, 0), (0, IN_WIDTH_PAD - IN_WIDTH))))
    lane = jnp.arange(LANES)
    vseg = _mx((lane[:, None] // GM_HEAD_DIM == lane[None, :] // GM_HEAD_DIM).astype(F32))
    ones = _mx(jnp.ones((LANES, LANES), F32))
    vgain = gm_v_gain.reshape(1, GM_WIDTH)
    wuq = w_uq.reshape(MLA_Q_RANK, MLA_HEADS, MLA_QK_DIM)
    wuq = _mx(jnp.pad(wuq, ((0, 0), (0, 0), (0, HEAD_PAD - MLA_QK_DIM))).reshape(MLA_Q_RANK, MLA_HEADS * HEAD_PAD))
    wukv = w_ukv.reshape(MLA_KV_RANK, MLA_HEADS, MLA_NOPE_DIM + MLA_V_DIM)
    wuk = jnp.pad(wukv[:, :, :MLA_NOPE_DIM], ((0, 0), (0, 0), (0, HEAD_PAD - MLA_NOPE_DIM)))
    wuk = _mx(wuk.reshape(MLA_KV_RANK, MLA_HEADS * HEAD_PAD))
    wuv = _mx(wukv[:, :, MLA_NOPE_DIM:].reshape(MLA_KV_RANK, MLA_WIDTH))
    r = jnp.arange(MLA_ROPE_DIM)
    eplace = jnp.zeros((LANES, MLA_HEADS, HEAD_PAD), F32).at[r, :, MLA_NOPE_DIM + r].set(1.0)
    eplace = _mx(eplace.reshape(LANES, MLA_HEADS * HEAD_PAD))
    half = MLA_ROPE_DIM // 2
    j = jnp.arange(half)
    rot = jnp.zeros((HEAD_PAD, HEAD_PAD), F32)
    rot = rot.at[MLA_NOPE_DIM + half + j, MLA_NOPE_DIM + j].set(-1.0).at[MLA_NOPE_DIM + j, MLA_NOPE_DIM + half + j].set(1.0)
    rot = _mx(rot)
    inv_freq = ROPE_THETA ** (-jnp.arange(half, dtype=F32) / half)
    invf = jnp.zeros((1, HEAD_PAD), F32).at[0, MLA_NOPE_DIM + j].set(inv_freq).at[0, MLA_NOPE_DIM + half + j].set(inv_freq)
    qg = jnp.pad(q_gain, (0, HEAD_PAD - MLA_QK_DIM)).reshape(1, HEAD_PAD)
    kg = jnp.pad(k_gain, (0, HEAD_PAD - MLA_QK_DIM)).reshape(1, HEAD_PAD)

    row = lambda w: pl.BlockSpec((tile, w), lambda i: (i, 0))
    consts = (g.reshape(1, D_MODEL), win, vseg, vgain, q_lat_gain.reshape(1, -1), wuq, kv_lat_gain.reshape(1, -1),
              wuk, wuv, eplace, rot, ones, qg, kg, invf)
    out_dt = MXU_DTYPE
    return pl.pallas_call(
        _mixer_in_kernel,
        grid=(n // tile,),
        in_specs=[row(D_MODEL), row(1)] + [_full(c.shape) for c in consts],
        out_specs=[row(GM_WIDTH), row(GM_WIDTH), row(MLA_HEADS * HEAD_PAD), row(MLA_HEADS * HEAD_PAD), row(MLA_WIDTH)],
        out_shape=[jax.ShapeDtypeStruct((n, GM_WIDTH), out_dt), jax.ShapeDtypeStruct((n, GM_WIDTH), out_dt),
                   jax.ShapeDtypeStruct((n, MLA_HEADS * HEAD_PAD), out_dt),
                   jax.ShapeDtypeStruct((n, MLA_HEADS * HEAD_PAD), out_dt),
                   jax.ShapeDtypeStruct((n, MLA_WIDTH), out_dt)],
        compiler_params=_params(1, 1),
        name="mixer_in",
    )(x2, pos2, *consts)


def _attention_kernel(q_ref, k_ref, v_ref, o_ref, *, tq):
    qi = pl.program_id(2)
    lane = lax.broadcasted_iota(jnp.int32, (tq, LANES), 1)
    heads = (slice(0, HEAD_PAD), slice(HEAD_PAD, 2 * HEAD_PAD))
    qs = [q_ref[:, hs] for hs in heads]

    def scores(j, hh):
        kj = k_ref[pl.ds(pl.multiple_of(j * tq, tq), tq), heads[hh]]
        return _dot_nt(qs[hh], kj)

    def update(carry, s, vj):
        m, l, acc = carry
        m_new = jnp.maximum(m, jnp.max(s, axis=-1, keepdims=True))
        alpha = jnp.exp(m - m_new)
        p = jnp.exp(s - m_new)
        acc = acc * alpha + _dot(_mx(p), vj)
        return m_new, l * alpha + jnp.sum(p, axis=-1, keepdims=True), acc

    def body(j, carry):
        vj = v_ref[pl.ds(pl.multiple_of(j * tq, tq), tq), :]
        return tuple(update(carry[hh], scores(j, hh), vj) for hh in range(2))

    init = (jnp.full((tq, 1), NEG, F32), jnp.zeros((tq, 1), F32), jnp.zeros((tq, LANES), F32))
    carry = lax.fori_loop(0, qi, body, (init, init))
    r = lax.broadcasted_iota(jnp.int32, (tq, tq), 0)
    c = lax.broadcasted_iota(jnp.int32, (tq, tq), 1)
    vd = v_ref[pl.ds(pl.multiple_of(qi * tq, tq), tq), :]
    outs = []
    for hh in range(2):
        m, l, acc = update(carry[hh], jnp.where(r >= c, scores(qi, hh), NEG), vd)
        outs.append(acc / l)
    o_ref[...] = jnp.where(lane < MLA_V_DIM, outs[0], outs[1]).astype(o_ref.dtype)


def _attention(q, k, v, batch, seq, tq):
    n = batch * seq
    nq = seq // tq
    return pl.pallas_call(
        functools.partial(_attention_kernel, tq=tq),
        grid=(batch, MLA_HEADS // 2, nq),
        in_specs=[pl.BlockSpec((tq, 2 * HEAD_PAD), lambda b, hp, i: (b * nq + i, hp)),
                  pl.BlockSpec((seq, 2 * HEAD_PAD), lambda b, hp, i: (b, hp)),
                  pl.BlockSpec((seq, 2 * MLA_V_DIM), lambda b, hp, i: (b, hp))],
        out_specs=pl.BlockSpec((tq, 2 * MLA_V_DIM), lambda b, hp, i: (b * nq + i, hp)),
        out_shape=jax.ShapeDtypeStruct((n, MLA_WIDTH), MXU_DTYPE),
        compiler_params=_params(3, 3),
        name="attention",
    )(q, k, v)


def _mixer_out_kernel(ug_ref, vn_ref, o_ref, x_ref, ws_ref, bias_ref, woa_ref, wob_ref, out_ref, a_ref, *, tile):
    lane = lax.broadcasted_iota(jnp.int32, (GM_CHUNK, LANES), 1)
    left = lane < GM_HEAD_DIM
    for c in range(tile // GM_CHUNK):
        rows = slice(c * GM_CHUNK, (c + 1) * GM_CHUNK)
        for p in range(GM_WIDTH // LANES):
            sl = slice(p * LANES, (p + 1) * LANES)
            vs = vn_ref[rows, sl]
            zero = jnp.zeros_like(vs)
            rhs = jnp.concatenate([jnp.where(left, vs, zero), jnp.where(left, zero, vs)], axis=0)
            mixed = _dot(ws_ref[p], rhs) + bias_ref[:, sl]
            a_ref[rows, sl] = (ug_ref[rows, sl].astype(F32) * mixed).astype(a_ref.dtype)
    out_ref[...] = x_ref[...] + _dot(a_ref[...], woa_ref[...]) + _dot(o_ref[...], wob_ref[...])


def _mixer_out(ug, vn, o, x2, gm_w_s, gm_b_s, w_out, tile):
    n = x2.shape[0]
    causal = jnp.tril(jnp.ones((GM_CHUNK, GM_CHUNK), bool))
    w = jnp.where(causal[None], gm_w_s, 0.0)
    ws = _mx(w.reshape(GM_HEADS // 2, 2, GM_CHUNK, GM_CHUNK).transpose(0, 2, 1, 3).reshape(GM_HEADS // 2, GM_CHUNK, 2 * GM_CHUNK))
    bias = jnp.repeat(gm_b_s.T, GM_HEAD_DIM, axis=1)
    woa = _mx(w_out[:GM_WIDTH])
    wob = _mx(w_out[GM_WIDTH:])
    row = lambda w_: pl.BlockSpec((tile, w_), lambda i: (i, 0))
    return pl.pallas_call(
        functools.partial(_mixer_out_kernel, tile=tile),
        grid=(n // tile,),
        in_specs=[row(GM_WIDTH), row(GM_WIDTH), row(MLA_WIDTH), row(D_MODEL),
                  _full(ws.shape), _full(bias.shape), _full(woa.shape), _full(wob.shape)],
        out_specs=row(D_MODEL),
        out_shape=jax.ShapeDtypeStruct((n, D_MODEL), F32),
        scratch_shapes=[pltpu.VMEM((tile, GM_WIDTH), MXU_DTYPE)],
        compiler_params=_params(1, 1),
        name="mixer_out",
    )(ug, vn, o, x2, ws, bias, woa, wob)


def _conformer_kernel(x_ref, g_ref, w1_ref, b1_ref, wdw_ref, bdw_ref, ng_ref, w2_ref, b2_ref, out_ref, zext_ref, shift_ref, *, tile):
    @pl.when(pl.program_id(1) == 0)
    def _():
        zext_ref[0:CONV_HALO, :] = jnp.zeros((CONV_HALO, CONV_CH), F32)
        zext_ref[CONV_HALO + tile:, :] = jnp.zeros((SUBLANES, CONV_CH), F32)

    x = x_ref[...]
    h = _mx(_rms(x, g_ref[...]))
    z = _dot(h, w1_ref[...]) + b1_ref[...]
    glu = z[:, :CONV_CH] * jax.nn.sigmoid(z[:, CONV_CH:])
    zext_ref[CONV_HALO:CONV_HALO + tile, :] = glu
    first = CONV_HALO - (CONV_WIDTH - 1)
    cols = []
    for cb in range(CONV_CH // LANES):
        sl = slice(cb * LANES, (cb + 1) * LANES)
        acc = jnp.zeros((tile, LANES), F32)
        for r in range(SUBLANES):
            taps = [k for k in range(CONV_WIDTH) if (first + k) % SUBLANES == r]
            shift_ref[r] = zext_ref[r:r + CONV_HALO + tile, sl]
            for k in taps:
                off = first + k - r
                acc = acc + shift_ref[r, off:off + tile, :] * wdw_ref[k:k + 1, sl]
        cols.append(acc)
    y = jnp.concatenate(cols, axis=1) + bdw_ref[...]
    zext_ref[0:CONV_HALO, :] = zext_ref[tile:tile + CONV_HALO, :]
    yn = _rms(y, ng_ref[...])
    s = yn * jax.nn.sigmoid(yn)
    out_ref[...] = x + _dot(_mx(s), w2_ref[...]) + b2_ref[...]


def _conformer(x2, g, w_pw1, b_pw1, w_dw, b_dw, norm_gain, w_pw2, b_pw2, batch, seq, tile):
    n = x2.shape[0]
    nt = seq // tile
    consts = (g.reshape(1, -1), _mx(w_pw1), b_pw1.reshape(1, -1), w_dw, b_dw.reshape(1, -1),
              norm_gain.reshape(1, -1), _mx(w_pw2), b_pw2.reshape(1, -1))
    row = pl.BlockSpec((tile, D_MODEL), lambda b, i: (b * nt + i, 0))
    return pl.pallas_call(
        functools.partial(_conformer_kernel, tile=tile),
        grid=(batch, nt),
        in_specs=[row] + [_full(c.shape) for c in consts],
        out_specs=row,
        out_shape=jax.ShapeDtypeStruct((n, D_MODEL), F32),
        scratch_shapes=[pltpu.VMEM((CONV_HALO + tile + SUBLANES, CONV_CH), F32),
                        pltpu.VMEM((SUBLANES, CONV_HALO + tile, LANES), F32)],
        compiler_params=_params(0, 2),
        name="conformer",
    )(x2, *consts)


def _oddeven_merge(lo, hi, r):
    step = r * 2
    if step < hi - lo:
        yield from _oddeven_merge(lo, hi, step)
        yield from _oddeven_merge(lo + r, hi, step)
        yield from [(i, i + r) for i in range(lo + r, hi - r, step)]
    else:
        yield (lo, lo + r)


def _oddeven_sort(lo, hi):
    if hi - lo >= 1:
        mid = lo + (hi - lo) // 2
        yield from _oddeven_sort(lo, mid)
        yield from _oddeven_sort(mid + 1, hi)
        yield from _oddeven_merge(lo, hi, 1)


SORT16 = tuple(_oddeven_sort(0, PEER_TOPK - 1))
BITONIC16 = tuple((i, i + d) for d in (8, 4, 2, 1) for i in range(PEER_TOPK) if not i & d)
SUBLANES = 8
EXPERT_I1_GROUP = 2
EXPERT_I2_PARTS = 1
EXPERT_DOT_I1 = 8
EXPERT_OUT_CHUNKS = 1


def _compare_exchange(b, i, j):
    b[i], b[j] = jnp.maximum(b[i], b[j]), jnp.minimum(b[i], b[j])


def _top16(blocks):
    b = list(blocks)
    for i, j in SORT16:
        _compare_exchange(b, i, j)
    for shift in (4, 2, 1):
        b = [jnp.maximum(b[i], pltpu.roll(b[PEER_TOPK - 1 - i], shift, axis=0)) for i in range(PEER_TOPK)]
        for i, j in BITONIC16:
            _compare_exchange(b, i, j)
    return b


def _all_sublanes(x, op):
    for shift in (4, 2, 1):
        x = op(x, pltpu.roll(x, shift, axis=0))
    return x


def _peer_route_kernel(x_ref, g_ref, wqt_ref, keys_ref, ht_ref, r2_ref, e2_ref, n_ref, coef_ref, *, tile):
    h = _rms(x_ref[...], g_ref[...])
    ht = _mx(h.T)
    ht_ref[...] = ht
    qt = _dot(wqt_ref[...], ht)
    row = lax.broadcasted_iota(jnp.int32, (SUBLANES, LANES), 0)
    nblk = PEER_N_KEYS // SUBLANES

    def stack(blocks8):
        out = blocks8[SUBLANES - 1]
        for r in range(SUBLANES - 2, -1, -1):
            out = jnp.where(row == r, blocks8[r], out)
        return out

    gate_dt = MXU_DTYPE
    pack = 4 // jnp.dtype(gate_dt).itemsize

    def store(ref, hd, ls, blocks):
        for i in range(0, nblk, pack):
            v = jnp.concatenate(blocks[i:i + pack], axis=0).astype(gate_dt)
            ref[hd, i // pack * SUBLANES:(i // pack + 1) * SUBLANES, ls] = v if pack == 1 else pltpu.bitcast(v, F32)

    def store_words(ref, hd, ls, blocks):
        for i in range(nblk):
            v = blocks[i]
            if pack == 2:
                bits = pltpu.bitcast(v.astype(gate_dt).astype(F32), jnp.uint32)
                v = pltpu.bitcast(bits | (bits >> 16), F32)
            ref[hd, i * SUBLANES:(i + 1) * SUBLANES, ls] = v

    for hd in range(PEER_HEADS):
        base = hd * 2 * PEER_HALF
        s1 = _dot(keys_ref[0, hd], _mx(qt[base:base + PEER_HALF]))
        s2 = _dot(keys_ref[1, hd], _mx(qt[base + PEER_HALF:base + 2 * PEER_HALF]))
        for lc in range(tile // LANES):
            ls = slice(lc * LANES, (lc + 1) * LANES)
            a1 = [s1[i * SUBLANES:(i + 1) * SUBLANES, ls] for i in range(nblk)]
            a2 = [s2[i * SUBLANES:(i + 1) * SUBLANES, ls] for i in range(nblk)]
            b1 = _top16(a1)
            b2 = _top16(a2)
            c1lo, c1hi = stack(b1[:SUBLANES]), stack(b1[SUBLANES:])
            c2lo, c2hi = stack(b2[:SUBLANES]), stack(b2[SUBLANES:])
            cand = []
            for a in range(4):
                cand += [b1[a] + c2lo, b1[a] + c2hi]
            c1lo_rest = jnp.where(row < 4, NEG, c1lo)
            for b in range(4):
                cand += [b2[b] + c1lo_rest, b2[b] + c1hi]
            top = _top16(cand)
            tau = top[PEER_TOPK - 1]
            zsum = jnp.ones_like(tau)
            for r in range(1, PEER_TOPK):
                zsum = zsum + jnp.exp(top[r] - top[0])
            inv_z = 1.0 / zsum
            theta = []
            for b in range(SUBLANES):
                v = jnp.minimum(jnp.where(c1lo + b2[b] >= tau, c1lo, -NEG), jnp.where(c1hi + b2[b] >= tau, c1hi, -NEG))
                theta.append(_all_sublanes(v, jnp.minimum))
            n_hi = _all_sublanes(jnp.where(b1[0] + c2hi >= tau, 1.0, 0.0), jnp.add)
            n_blocks, coef_blocks, r2_blocks, e2_blocks = [], [], [], []
            for i in range(nblk):
                cnt = jnp.zeros((SUBLANES, LANES), F32)
                for b in range(SUBLANES):
                    cnt = jnp.where(a1[i] >= theta[b], float(b + 1), cnt)
                n_blocks.append(cnt + jnp.where(a1[i] == b1[0], n_hi, 0.0))
                coef_blocks.append(jnp.exp(a1[i] - b1[0]) * inv_z)
                rank = jnp.full((SUBLANES, LANES), float(PEER_TOPK), F32)
                for b in range(PEER_TOPK - 1, -1, -1):
                    rank = jnp.where(a2[i] >= b2[b], float(b), rank)
                r2_blocks.append(rank)
                e2_blocks.append(jnp.exp(a2[i] - b2[0]))
            store_words(n_ref, hd, ls, n_blocks)
            store_words(coef_ref, hd, ls, coef_blocks)
            store(r2_ref, hd, ls, r2_blocks)
            store(e2_ref, hd, ls, e2_blocks)


def _peer_route(x2, g, w_q, sub_keys, tile):
    n = x2.shape[0]
    wqt = _mx(w_q.T)
    keys = _mx(sub_keys)
    pack = 4 // jnp.dtype(MXU_DTYPE).itemsize
    tok = pl.BlockSpec((PEER_HEADS, PEER_N_KEYS, tile), lambda i: (0, 0, i))
    tokp = pl.BlockSpec((PEER_HEADS, PEER_N_KEYS // pack, tile), lambda i: (0, 0, i))
    packed = jax.ShapeDtypeStruct((PEER_HEADS, PEER_N_KEYS // pack, n), F32)
    words = jax.ShapeDtypeStruct((PEER_HEADS, PEER_N_KEYS, n), F32)
    return pl.pallas_call(
        functools.partial(_peer_route_kernel, tile=tile),
        grid=(n // tile,),
        in_specs=[pl.BlockSpec((tile, D_MODEL), lambda i: (i, 0)), _full((1, D_MODEL)), _full(wqt.shape), _full(keys.shape)],
        out_specs=[pl.BlockSpec((D_MODEL, tile), lambda i: (0, i)), tokp, tokp, tok, tok],
        out_shape=[jax.ShapeDtypeStruct((D_MODEL, n), MXU_DTYPE), packed, packed, words, words],
        compiler_params=_params(1, 1),
        name="peer_route",
    )(x2, g.reshape(1, D_MODEL), wqt, keys)


def _peer_expert_kernel(ht_ref, u_ref, vt_ref, n_ref, coef_ref, r2_ref, e2_ref, x_ref, out_ref,
                        acc_ref, g_ref, a_ref, *, tile, ib):
    step = pl.program_id(1)

    @pl.when(step == 0)
    def _():
        acc_ref[...] = jnp.zeros_like(acc_ref)

    gate_dt = g_ref.dtype
    pack = 4 // jnp.dtype(gate_dt).itemsize
    rows_per_vreg = SUBLANES * pack
    part = PEER_N_KEYS // EXPERT_I2_PARTS
    shape3 = (part // rows_per_vreg, rows_per_vreg, LANES)
    zero = jnp.zeros(shape3, gate_dt)

    ht = ht_ref[...]

    def row_vreg(ref, hd, k, ls):
        word = jnp.broadcast_to(ref[hd, k:k + 1, ls], (SUBLANES, LANES))
        return word if pack == 1 else pltpu.bitcast(word, gate_dt)

    def table(ref, hd, p, ls):
        words = ref[hd, p * part // pack:(p + 1) * part // pack, ls]
        return (words if pack == 1 else pltpu.bitcast(words, gate_dt)).reshape(shape3)

    n_groups = ib // EXPERT_I1_GROUP
    groups_per_chunk = n_groups // EXPERT_OUT_CHUNKS

    for kg in range(n_groups):
        ks = range(kg * EXPERT_I1_GROUP, (kg + 1) * EXPERT_I1_GROUP)
        if ks[0] % EXPERT_DOT_I1 == 0:
            drows = slice(ks[0] * PEER_N_KEYS, (ks[0] + EXPERT_DOT_I1) * PEER_N_KEYS)
            a_ref[drows, :] = _gelu(_dot(u_ref[drows, :], ht).astype(gate_dt))
        for lc in range(tile // LANES):
            ls = slice(lc * LANES, (lc + 1) * LANES)
            for p in range(EXPERT_I2_PARTS):
                w = {k: zero for k in ks}
                for hd in range(PEER_HEADS):
                    r2 = table(r2_ref, hd, p, ls)
                    e2 = table(e2_ref, hd, p, ls)
                    for k in ks:
                        hit = r2 < row_vreg(n_ref, hd, k, ls)[None]
                        w[k] = w[k] + jnp.where(hit, e2, zero) * row_vreg(coef_ref, hd, k, ls)[None]
                for k in ks:
                    rows = slice(k * PEER_N_KEYS + p * part, k * PEER_N_KEYS + (p + 1) * part)
                    g_ref[rows, ls] = (w[k] * a_ref[rows, ls].reshape(shape3)).reshape(part, LANES)
        if (kg + 1) % groups_per_chunk == 0:
            c = (kg + 1) // groups_per_chunk - 1
            crows = slice(c * groups_per_chunk * EXPERT_I1_GROUP * PEER_N_KEYS,
                          (c + 1) * groups_per_chunk * EXPERT_I1_GROUP * PEER_N_KEYS)
            acc_ref[...] += _dot(vt_ref[:, crows], g_ref[crows, :])

    @pl.when(step == pl.num_programs(1) - 1)
    def _():
        out_ref[...] = x_ref[...] + acc_ref[...].T


def _peer_experts(x2, ht, r2, e2, cnt, coef, u, vt, tile, ib):
    n = x2.shape[0]
    blk = ib * PEER_N_KEYS
    once = pl.Buffered(1)
    small = pl.BlockSpec((PEER_HEADS, ib, tile), lambda i, j: (0, j, i))
    big = pl.BlockSpec((PEER_HEADS, r2.shape[1], tile), lambda i, j: (0, 0, i), pipeline_mode=once)
    return pl.pallas_call(
        functools.partial(_peer_expert_kernel, tile=tile, ib=ib),
        grid=(n // tile, PEER_N_KEYS // ib),
        in_specs=[pl.BlockSpec((D_MODEL, tile), lambda i, j: (0, i), pipeline_mode=once),
                  pl.BlockSpec((blk, D_MODEL), lambda i, j: (j, 0)),
                  pl.BlockSpec((D_MODEL, blk), lambda i, j: (0, j)),
                  small, small, big, big,
                  pl.BlockSpec((tile, D_MODEL), lambda i, j: (i, 0), pipeline_mode=once)],
        out_specs=pl.BlockSpec((tile, D_MODEL), lambda i, j: (i, 0)),
        out_shape=jax.ShapeDtypeStruct((n, D_MODEL), F32),
        scratch_shapes=[pltpu.VMEM((D_MODEL, tile), F32), pltpu.VMEM((blk, tile), MXU_DTYPE),
                        pltpu.VMEM((blk, tile), MXU_DTYPE)],
        compiler_params=_params(1, 2),
        name="peer_experts",
    )(ht, u, vt, cnt, coef, r2, e2, x2)


def _peer(x2, g, w_q, sub_keys, expert_u, expert_v, route_tile, tile, ib):
    ht, r2, e2, cnt, coef = _peer_route(x2, g, w_q, sub_keys, route_tile)
    return _peer_experts(x2, ht, r2, e2, cnt, coef, _mx(expert_u), _mx(expert_v.T), tile, ib)


def _tile(n, want):
    t = min(n, want)
    assert n % t == 0, (n, t)
    return t


def kernel(x, positions, mix_norm_even, w_in, gm_v_gain, gm_w_s, gm_b_s, mla_q_lat_gain, mla_w_uq, mla_kv_lat_gain, mla_w_ukv, mla_q_gain, mla_k_gain, w_out, mix_norm_odd, conv_w_pw1, conv_b_pw1, conv_w_dw, conv_b_dw, conv_norm_gain, conv_w_pw2, conv_b_pw2, ffn_norm, peer_w_q, peer_sub_keys, peer_u, peer_v):
    batch, seq, d = x.shape
    n = batch * seq
    depth = ffn_norm.shape[0]
    x2 = x.reshape(n, d)
    pos2 = positions.reshape(n, 1)
    row_tile = _tile(n, 512)
    for layer in range(depth):
        i = layer // 2
        if layer % 2 == 0:
            ug, vn, q, k, v = _mixer_in(x2, pos2, mix_norm_even[i], w_in[i], gm_v_gain[i], mla_q_lat_gain[i],
                                        mla_w_uq[i], mla_kv_lat_gain[i], mla_w_ukv[i], mla_q_gain[i],
                                        mla_k_gain[i], row_tile)
            o = _attention(q, k, v, batch, seq, _tile(seq, 512))
            x2 = _mixer_out(ug, vn, o, x2, gm_w_s[i], gm_b_s[i], w_out[i], row_tile)
        else:
            x2 = _conformer(x2, mix_norm_odd[i], conv_w_pw1[i], conv_b_pw1[i], conv_w_dw[i], conv_b_dw[i],
                            conv_norm_gain[i], conv_w_pw2[i], conv_b_pw2[i], batch, seq, _tile(seq, 256))
        x2 = _peer(x2, ffn_norm[layer], peer_w_q[layer], peer_sub_keys[layer], peer_u[layer], peer_v[layer],
                   _tile(n, 256), row_tile, 32)
    return x2.reshape(batch, seq, d)
```
